```python
import functools
import jax, jax.numpy as jnp
from jax import lax
import numpy as np

D_MODEL = 4096
BATCH = 8
SEQ = 2048
DEPTH = 2
DEC_BATCH = 16
DEC_SEQ = 32
PAST_LEN = 2048

CHUNK = 64
LEFT_CHUNKS = 8
BAND_CHUNKS = LEFT_CHUNKS + 1
ATT_PAST = LEFT_CHUNKS * CHUNK
WIDTH_ATT = D_MODEL // 2
WIDTH_SC = D_MODEL // 4
WIDTH_CV = D_MODEL - WIDTH_ATT - WIDTH_SC
HEAD_DIM = 128
N_HEADS_ATT = WIDTH_ATT // HEAD_DIM
MAX_REL = 256
N_REL = 2 * MAX_REL + 1
SC_WIDTH = 3
CV_WIDTH = 31
D_FF = -(-8 * D_MODEL // (3 * 256)) * 256
SPLIT_SIZES = [WIDTH_ATT] * 3 + [WIDTH_SC] * 3 + [WIDTH_CV] * 2
D_IN = sum(SPLIT_SIZES)
SPLITS = [int(s) for s in np.cumsum(SPLIT_SIZES)[:-1]]
EPS = 1e-6

kernel_name = "hybrid_chunk_stream_encoder_step"


def rmsnorm(x, g):
    xf = x.astype(jnp.float32)
    y = xf * lax.rsqrt(jnp.mean(xf * xf, axis=-1, keepdims=True) + EPS)
    return (y * g.astype(jnp.float32)).astype(x.dtype)


def layernorm(x, g, b):
    xf = x.astype(jnp.float32)
    mu = jnp.mean(xf, axis=-1, keepdims=True)
    var = jnp.mean(jnp.square(xf - mu), axis=-1, keepdims=True)
    y = (xf - mu) * lax.rsqrt(var + EPS)
    return (y * g.astype(jnp.float32) + b.astype(jnp.float32)).astype(x.dtype)


def causal_dwconv(u, hist, w):
    full = jnp.concatenate([hist.astype(u.dtype), u], axis=1)
    y = lax.conv_general_dilated(full, w[:, None, :].astype(u.dtype), (1,), 'VALID',
                                 dimension_numbers=('NWC', 'WIO', 'NWC'),
                                 feature_group_count=u.shape[-1])
    return y, full[:, full.shape[1] - (w.shape[0] - 1):]


def band_attention(q, k, v, n_past, rel_bias, key_valid):
    tq, tk = q.shape[2], k.shape[2]
    dist = jnp.arange(tq)[:, None] + n_past - jnp.arange(tk)[None, :]
    bias = rel_bias[:, jnp.clip(dist, -MAX_REL, MAX_REL) + MAX_REL].astype(jnp.float32)
    s = jnp.einsum('bnqhd,bnkhd->bnhqk', q, k, preferred_element_type=jnp.float32) * (HEAD_DIM ** -0.5) + bias
    if key_valid is not None:
        s = jnp.where(key_valid[None, :, None, None, :], s, jnp.finfo(jnp.float32).min)
    p = jax.nn.softmax(s, axis=-1).astype(v.dtype)
    return jnp.einsum('bnhqk,bnkhd->bnqhd', p, v)


def prompt_attention(q, k, v, rel_bias):
    b, s = q.shape[:2]
    nc = s // CHUNK
    chunk = lambda t: t.reshape(b, nc, CHUNK, N_HEADS_ATT, HEAD_DIM)
    pad = ((0, 0), (LEFT_CHUNKS, 0), (0, 0), (0, 0), (0, 0))
    kc, vc = jnp.pad(chunk(k), pad), jnp.pad(chunk(v), pad)
    idx = jnp.arange(nc)[:, None] + jnp.arange(BAND_CHUNKS)[None, :]
    band = BAND_CHUNKS * CHUNK
    kb = kc[:, idx].reshape(b, nc, band, N_HEADS_ATT, HEAD_DIM)
    vb = vc[:, idx].reshape(b, nc, band, N_HEADS_ATT, HEAD_DIM)
    key_pos = (jnp.arange(nc)[:, None] - LEFT_CHUNKS) * CHUNK + jnp.arange(band)[None, :]
    o = band_attention(chunk(q), kb, vb, ATT_PAST, rel_bias, key_pos >= 0)
    return o.reshape(b, s, WIDTH_ATT)


def sample_attention(q, k, v, cache_k, cache_v, rel_bias):
    b, t = q.shape[:2]
    kf = jnp.concatenate([cache_k.astype(k.dtype), k], axis=1)[:, None]
    vf = jnp.concatenate([cache_v.astype(v.dtype), v], axis=1)[:, None]
    o = band_attention(q[:, None], kf, vf, cache_k.shape[1], rel_bias, None)
    return o[:, 0].reshape(b, t, WIDTH_ATT)


def trunk_layer(x, attend, hist_b, hist_c, g_mix, w_in_l, conv_b_w_l, conv_c_w_l, conv_c_b_l,
                ln_g, ln_b, w_out_l, g_ffn, w_gate, w_up, w_down):
    b, t, _ = x.shape
    h = rmsnorm(x, g_mix)
    z = h @ w_in_l
    q, k, v, sb, sc, sh, ca, cg = jnp.split(z, SPLITS, axis=-1)
    heads = lambda a: a.reshape(b, t, N_HEADS_ATT, HEAD_DIM)
    q, k, v = heads(q), heads(k), heads(v)
    y_att = attend(q, k, v)
    zb, new_hb = causal_dwconv(sc * sh, hist_b, conv_b_w_l)
    y_sc = sb * zb
    zc, new_hc = causal_dwconv(ca * jax.nn.sigmoid(cg), hist_c, conv_c_w_l)
    y_cv = jax.nn.silu(layernorm(zc + conv_c_b_l, ln_g, ln_b))
    x = x + jnp.concatenate([y_att, y_sc, y_cv], axis=-1) @ w_out_l
    h = rmsnorm(x, g_ffn)
    x = x + (jax.nn.silu(h @ w_gate) * (h @ w_up)) @ w_down
    return x, k, v, new_hb, new_hc


def setup_inputs(seed: int = 0) -> dict:
    key = jax.random.key(seed)
    ks = jax.random.split(key, 24)
    nrm = lambda k, shape, scale: jax.random.normal(k, shape, jnp.float32) * scale
    n_att = min(ATT_PAST, PAST_LEN)
    return {
        "x_prompt": nrm(ks[0], (BATCH, SEQ, D_MODEL), 1.0),
        "x_sample": nrm(ks[1], (DEC_BATCH, DEC_SEQ, D_MODEL), 1.0),
        "cache_attn_k": nrm(ks[2], (DEPTH, DEC_BATCH, n_att, N_HEADS_ATT, HEAD_DIM), 1.0),
        "cache_attn_v": nrm(ks[3], (DEPTH, DEC_BATCH, n_att, N_HEADS_ATT, HEAD_DIM), 1.0),
        "cache_conv_b": nrm(ks[4], (DEPTH, DEC_BATCH, SC_WIDTH - 1, WIDTH_SC), 1.0),
        "cache_conv_c": nrm(ks[5], (DEPTH, DEC_BATCH, CV_WIDTH - 1, WIDTH_CV), 0.5),
        "norm_mix_g": 1.0 + nrm(ks[6], (DEPTH, D_MODEL), 0.01),
        "w_in": nrm(ks[7], (DEPTH, D_MODEL, D_IN), D_MODEL ** -0.5),
        "rel_bias": nrm(ks[8], (DEPTH, N_HEADS_ATT, N_REL), 0.1),
        "conv_b_w": nrm(ks[9], (DEPTH, SC_WIDTH, WIDTH_SC), SC_WIDTH ** -0.5),
        "conv_c_w": nrm(ks[10], (DEPTH, CV_WIDTH, WIDTH_CV), CV_WIDTH ** -0.5),
        "conv_c_b": nrm(ks[11], (DEPTH, WIDTH_CV), 0.01),
        "ln_c_g": 1.0 + nrm(ks[12], (DEPTH, WIDTH_CV), 0.01),
        "ln_c_b": nrm(ks[13], (DEPTH, WIDTH_CV), 0.01),
        "w_out": nrm(ks[14], (DEPTH, D_MODEL, D_MODEL), D_MODEL ** -0.5),
        "norm_ffn_g": 1.0 + nrm(ks[15], (DEPTH, D_MODEL), 0.01),
        "w_ffn_gate": nrm(ks[16], (DEPTH, D_MODEL, D_FF), D_MODEL ** -0.5),
        "w_ffn_up": nrm(ks[17], (DEPTH, D_MODEL, D_FF), D_MODEL ** -0.5),
        "w_ffn_down": nrm(ks[18], (DEPTH, D_FF, D_MODEL), D_FF ** -0.5),
        "final_norm_g": 1.0 + nrm(ks[19], (D_MODEL,), 0.01),
    }


def reference(x_prompt, x_sample, cache_attn_k, cache_attn_v, cache_conv_b, cache_conv_c,
              norm_mix_g, w_in, rel_bias, conv_b_w, conv_c_w, conv_c_b, ln_c_g, ln_c_b,
              w_out, norm_ffn_g, w_ffn_gate, w_ffn_up, w_ffn_down, final_norm_g):
    xp, xs = x_prompt, x_sample
    bp, sp = xp.shape[0], xp.shape[1]
    n_keep = min(ATT_PAST, sp)
    pk, pv, pb, pc, sk, sv, sb_, sc_ = [], [], [], [], [], [], [], []
    for l in range(DEPTH):
        shared = (norm_mix_g[l], w_in[l], conv_b_w[l], conv_c_w[l], conv_c_b[l], ln_c_g[l], ln_c_b[l],
                  w_out[l], norm_ffn_g[l], w_ffn_gate[l], w_ffn_up[l], w_ffn_down[l])
        att_p = functools.partial(prompt_attention, rel_bias=rel_bias[l])
        hb0 = jnp.zeros((bp, SC_WIDTH - 1, WIDTH_SC), xp.dtype)
        hc0 = jnp.zeros((bp, CV_WIDTH - 1, WIDTH_CV), xp.dtype)
        xp, k_p, v_p, hb_p, hc_p = trunk_layer(xp, att_p, hb0, hc0, *shared)
        pk.append(k_p[:, sp - n_keep:])
        pv.append(v_p[:, sp - n_keep:])
        pb.append(hb_p)
        pc.append(hc_p)
        att_s = functools.partial(sample_attention, cache_k=cache_attn_k[l], cache_v=cache_attn_v[l],
                                  rel_bias=rel_bias[l])
        xs, k_s, v_s, hb_s, hc_s = trunk_layer(xs, att_s, cache_conv_b[l], cache_conv_c[l], *shared)
        sk.append(k_s)
        sv.append(v_s)
        sb_.append(hb_s)
        sc_.append(hc_s)
    y_prompt = rmsnorm(xp, final_norm_g)
    y_sample = rmsnorm(xs, final_norm_g)
    return (y_prompt, y_sample,
            jnp.stack(pk), jnp.stack(pv), jnp.stack(pb), jnp.stack(pc),
            jnp.stack(sk), jnp.stack(sv), jnp.stack(sb_), jnp.stack(sc_))
```

```python
import functools

import jax
import jax.numpy as jnp
from jax import lax
from jax.experimental import pallas as pl
from jax.experimental.pallas import tpu as pltpu

F32 = jnp.float32
BF16 = jnp.bfloat16

D_MODEL = 4096
BATCH = 8
SEQ = 2048
DEPTH = 2
DEC_BATCH = 16
DEC_SEQ = 32
CHUNK = 64
LEFT_CHUNKS = 8
ATT_PAST = LEFT_CHUNKS * CHUNK
WIDTH_ATT = 2048
WIDTH_SC = 1024
WIDTH_CV = 1024
HEAD_DIM = 128
N_HEADS = 16
MAX_REL = 256
SC_WIDTH = 3
CV_WIDTH = 31
D_FF = 11008
D_IN = 11264
EPS = 1e-6

ROWS_P = BATCH * SEQ
ROWS_S = DEC_BATCH * DEC_SEQ
ROWS = ROWS_P + ROWS_S

TM = 512
TN = 1024
TF = 256
NORM_ROWS = 256
ATT_Q = 128
ATT_KB = ATT_PAST // ATT_Q + 1
MIX_ROWS = 256
CONV_CHUNK = 32
HB_PAD = 8
HC_PAD = 32
NEG = -1e30
VMEM_LIMIT = 56 * 1024 * 1024

ZR_SB, ZR_SC, ZR_SH, ZR_CA, ZR_CG = 2, 3, 4, 5, 6


def _params(sem):
    return pltpu.CompilerParams(dimension_semantics=sem, vmem_limit_bytes=VMEM_LIMIT)


def _rmsnorm_kernel(x_ref, g_ref, o_ref):
    x = x_ref[...]
    ms = jnp.mean(x * x, axis=-1, keepdims=True)
    o_ref[...] = (x * lax.rsqrt(ms + EPS) * g_ref[...]).astype(o_ref.dtype)


def _rmsnorm(x, g, out_dtype, row0, rows):
    blk0 = row0 // NORM_ROWS
    return pl.pallas_call(
        _rmsnorm_kernel,
        grid=(rows // NORM_ROWS,),
        in_specs=[pl.BlockSpec((NORM_ROWS, D_MODEL), lambda i: (i + blk0, 0)),
                  pl.BlockSpec((1, D_MODEL), lambda i: (0, 0))],
        out_specs=pl.BlockSpec((NORM_ROWS, D_MODEL), lambda i: (i, 0)),
        out_shape=jax.ShapeDtypeStruct((rows, D_MODEL), out_dtype),
        compiler_params=_params(("parallel",)),
        name="rmsnorm",
    )(x, g.reshape(1, D_MODEL))


def _inproj_kv_kernel(a_ref, w_ref, z_ref, kv_ref):
    r = jnp.dot(a_ref[...], w_ref[...], preferred_element_type=F32)
    kv_ref[...] = r
    z_ref[...] = r.astype(BF16)


def _inproj_rest_kernel(a_ref, w_ref, z_ref):
    z_ref[...] = jnp.dot(a_ref[...], w_ref[...], preferred_element_type=F32).astype(BF16)


def _inproj(h, w_in_b):
    a_spec = pl.BlockSpec((TM, D_MODEL), lambda j, i: (i, 0))
    o_spec = pl.BlockSpec((TM, TN), lambda j, i: (i, j))
    kv_blk0 = WIDTH_ATT // TN
    n_kv = 2 * WIDTH_ATT // TN
    z_kv, kv = pl.pallas_call(
        _inproj_kv_kernel,
        grid=(n_kv, ROWS // TM),
        in_specs=[a_spec, pl.BlockSpec((D_MODEL, TN), lambda j, i: (0, j + kv_blk0))],
        out_specs=[o_spec, o_spec],
        out_shape=[jax.ShapeDtypeStruct((ROWS, 2 * WIDTH_ATT), BF16),
                   jax.ShapeDtypeStruct((ROWS, 2 * WIDTH_ATT), F32)],
        compiler_params=_params(("parallel", "parallel")),
        name="inproj_kv",
    )(h, w_in_b)
    n_rest = (D_IN - 2 * WIDTH_ATT) // TN
    z_rest = pl.pallas_call(
        _inproj_rest_kernel,
        grid=(n_rest, ROWS // TM),
        in_specs=[a_spec,
                  pl.BlockSpec((D_MODEL, TN), lambda j, i: (0, jnp.where(j < kv_blk0, j, j + n_kv)))],
        out_specs=o_spec,
        out_shape=jax.ShapeDtypeStruct((ROWS, D_IN - 2 * WIDTH_ATT), BF16),
        compiler_params=_params(("parallel", "parallel")),
        name="inproj_rest",
    )(h, w_in_b)
    return z_kv, kv, z_rest


def _softmax_pv(parts, vals):
    m = parts[0].max(axis=-1, keepdims=True)
    for s in parts[1:]:
        m = jnp.maximum(m, s.max(axis=-1, keepdims=True))
    acc = None
    l = None
    for s, v in zip(parts, vals):
        p = jnp.exp(s - m)
        ls = p.sum(axis=-1, keepdims=True)
        o = jnp.dot(p.astype(BF16), v, preferred_element_type=F32)
        acc = o if acc is None else acc + o
        l = ls if l is None else l + ls
    return acc / l


def _qk(q, k):
    return lax.dot_general(q, k, (((1,), (1,)), ((), ())), preferred_element_type=F32)


def _attn_prompt_kernel(q_ref, *refs):
    k_refs = refs[:ATT_KB]
    v_refs = refs[ATT_KB:2 * ATT_KB]
    bias_ref = refs[2 * ATT_KB]
    o_ref = refs[2 * ATT_KB + 1]
    blk = pl.program_id(0) % (SEQ // ATT_Q)
    col = lax.broadcasted_iota(jnp.int32, (1, ATT_KB * ATT_Q), 1)
    pen = jnp.where(col >= (ATT_KB - 1 - blk) * ATT_Q, 0.0, NEG).astype(F32)
    scale = HEAD_DIM ** -0.5
    for h in range(N_HEADS):
        hs = slice(h * HEAD_DIM, (h + 1) * HEAD_DIM)
        k_all = jnp.concatenate([r[:, hs] for r in k_refs], axis=0)
        v_all = jnp.concatenate([r[:, hs] for r in v_refs], axis=0)
        s = _qk(q_ref[:, hs], k_all) * scale + bias_ref[h] + pen
        o_ref[:, hs] = _softmax_pv([s], [v_all]).astype(o_ref.dtype)


def _attn_prompt(z_rest, z_kv, bias):
    blocks_per_seq = SEQ // ATT_Q

    def kv_map(d, col):
        def index(n):
            first = (n // blocks_per_seq) * blocks_per_seq
            return (jnp.maximum(n - (ATT_KB - 1) + d, first), col)
        return index

    blk = (ATT_Q, WIDTH_ATT)
    in_specs = [pl.BlockSpec(blk, lambda n: (n, 0))]
    in_specs += [pl.BlockSpec(blk, kv_map(d, 0)) for d in range(ATT_KB)]
    in_specs += [pl.BlockSpec(blk, kv_map(d, 1)) for d in range(ATT_KB)]
    in_specs += [pl.BlockSpec((N_HEADS, ATT_Q, ATT_KB * ATT_Q), lambda n: (0, 0, 0))]
    return pl.pallas_call(
        _attn_prompt_kernel,
        grid=(ROWS_P // ATT_Q,),
        in_specs=in_specs,
        out_specs=pl.BlockSpec(blk, lambda n: (n, 0)),
        out_shape=jax.ShapeDtypeStruct((ROWS, WIDTH_ATT), BF16),
        compiler_params=_params(("parallel",)),
        name="attn_prompt",
    )(z_rest, *([z_kv] * (2 * ATT_KB)), bias)


def _attn_sample_kernel(q_ref, kn_ref, vn_ref, ck_ref, cv_ref, bias_c_ref, bias_n_ref, y_hbm, o_ref):
    del y_hbm
    scale = HEAD_DIM ** -0.5
    for h in range(N_HEADS):
        hs = slice(h * HEAD_DIM, (h + 1) * HEAD_DIM)
        q = q_ref[:, hs]
        s_c = _qk(q, ck_ref[0, :, hs].astype(BF16)) * scale + bias_c_ref[h]
        s_n = _qk(q, kn_ref[:, hs]) * scale + bias_n_ref[h]
        o = _softmax_pv([s_c, s_n], [cv_ref[0, :, hs].astype(BF16), vn_ref[:, hs]])
        o_ref[:, hs] = o.astype(o_ref.dtype)


def _attn_sample(z_rest, z_kv, cache_k, cache_v, bias_c, bias_n, y_att):
    blk0 = ROWS_P // DEC_SEQ
    n_cache = cache_k.shape[1]
    blk = (DEC_SEQ, WIDTH_ATT)
    cache_spec = pl.BlockSpec((1, n_cache, WIDTH_ATT), lambda b: (b, 0, 0))
    return pl.pallas_call(
        _attn_sample_kernel,
        grid=(DEC_BATCH,),
        in_specs=[pl.BlockSpec(blk, lambda b: (b + blk0, 0)),
                  pl.BlockSpec(blk, lambda b: (b + blk0, 0)),
                  pl.BlockSpec(blk, lambda b: (b + blk0, 1)),
                  cache_spec, cache_spec,
                  pl.BlockSpec((N_HEADS, DEC_SEQ, n_cache), lambda b: (0, 0, 0)),
                  pl.BlockSpec((N_HEADS, DEC_SEQ, DEC_SEQ), lambda b: (0, 0, 0)),
                  pl.BlockSpec(memory_space=pl.ANY)],
        out_specs=pl.BlockSpec(blk, lambda b: (b + blk0, 0)),
        out_shape=jax.ShapeDtypeStruct((ROWS, WIDTH_ATT), BF16),
        input_output_aliases={7: 0},
        compiler_params=_params(("parallel",)),
        name="attn_sample",
    )(z_rest, z_kv, z_kv, cache_k, cache_v, bias_c, bias_n, y_att)


def _mixb_body(sb_ref, sc_ref, sh_ref, w_ref, o_ref, hist_out_ref, scr, rows):
    scr[HB_PAD:, :] = sc_ref[...].astype(F32) * sh_ref[...].astype(F32)
    for r0 in range(0, rows, CONV_CHUNK):
        acc = None
        for t in range(SC_WIDTH):
            start = r0 + HB_PAD - (SC_WIDTH - 1) + t
            term = scr[start:start + CONV_CHUNK, :] * w_ref[t:t + 1, :]
            acc = term if acc is None else acc + term
        y = sb_ref[r0:r0 + CONV_CHUNK, :].astype(F32) * acc
        o_ref[r0:r0 + CONV_CHUNK, :] = y.astype(o_ref.dtype)
    hist_out_ref[0] = scr[rows:rows + HB_PAD, :]


def _mixb_prompt_kernel(sb_ref, sc_ref, sh_ref, scp_ref, shp_ref, w_ref, o_ref, hist_out_ref, scr):
    first = pl.program_id(0) % (SEQ // MIX_ROWS) == 0
    prev = scp_ref[...].astype(F32) * shp_ref[...].astype(F32)
    prev = jnp.where(first, 0.0, prev)
    scr[0:HB_PAD, :] = prev[prev.shape[0] - HB_PAD:, :]
    _mixb_body(sb_ref, sc_ref, sh_ref, w_ref, o_ref, hist_out_ref, scr, MIX_ROWS)


def _mixb_sample_kernel(sb_ref, sc_ref, sh_ref, hist_ref, w_ref, y_hbm, o_ref, hist_out_ref, scr):
    del y_hbm
    scr[0:HB_PAD, :] = hist_ref[0]
    _mixb_body(sb_ref, sc_ref, sh_ref, w_ref, o_ref, hist_out_ref, scr, DEC_SEQ)


def _mixb_prompt(z_rest, w):
    halo = 16
    tiles_per_seq = SEQ // MIX_ROWS
    cur = lambda c: pl.BlockSpec((MIX_ROWS, WIDTH_SC), lambda i: (i, c))
    prev = lambda c: pl.BlockSpec((halo, WIDTH_SC),
                                  lambda i: (jnp.maximum(i * (MIX_ROWS // halo) - 1, 0), c))
    return pl.pallas_call(
        _mixb_prompt_kernel,
        grid=(ROWS_P // MIX_ROWS,),
        in_specs=[cur(ZR_SB), cur(ZR_SC), cur(ZR_SH), prev(ZR_SC), prev(ZR_SH),
                  pl.BlockSpec((SC_WIDTH, WIDTH_SC), lambda i: (0, 0))],
        out_specs=[pl.BlockSpec((MIX_ROWS, WIDTH_SC), lambda i: (i, 0)),
                   pl.BlockSpec((1, HB_PAD, WIDTH_SC), lambda i: (i // tiles_per_seq, 0, 0))],
        out_shape=[jax.ShapeDtypeStruct((ROWS, WIDTH_SC), BF16),
                   jax.ShapeDtypeStruct((BATCH, HB_PAD, WIDTH_SC), F32)],
        scratch_shapes=[pltpu.VMEM((HB_PAD + MIX_ROWS, WIDTH_SC), F32)],
        compiler_params=_params(("arbitrary",)),
        name="mixb_prompt",
    )(z_rest, z_rest, z_rest, z_rest, z_rest, w)


def _mixb_sample(z_rest, hist, w, y_sc):
    blk0 = ROWS_P // DEC_SEQ
    cur = lambda c: pl.BlockSpec((DEC_SEQ, WIDTH_SC), lambda b: (b + blk0, c))
    return pl.pallas_call(
        _mixb_sample_kernel,
        grid=(DEC_BATCH,),
        in_specs=[cur(ZR_SB), cur(ZR_SC), cur(ZR_SH),
                  pl.BlockSpec((1, HB_PAD, WIDTH_SC), lambda b: (b, 0, 0)),
                  pl.BlockSpec((SC_WIDTH, WIDTH_SC), lambda b: (0, 0)),
                  pl.BlockSpec(memory_space=pl.ANY)],
        out_specs=[pl.BlockSpec((DEC_SEQ, WIDTH_SC), lambda b: (b + blk0, 0)),
                   pl.BlockSpec((1, HB_PAD, WIDTH_SC), lambda b: (b, 0, 0))],
        out_shape=[jax.ShapeDtypeStruct((ROWS, WIDTH_SC), BF16),
                   jax.ShapeDtypeStruct((DEC_BATCH, HB_PAD, WIDTH_SC), F32)],
        scratch_shapes=[pltpu.VMEM((HB_PAD + DEC_SEQ, WIDTH_SC), F32)],
        input_output_aliases={5: 0},
        compiler_params=_params(("arbitrary",)),
        name="mixb_sample",
    )(z_rest, z_rest, z_rest, hist, w, y_sc)


def _glu(a, g):
    return a.astype(F32) * jax.nn.sigmoid(g.astype(F32))


def _mixc_body(ca_ref, cg_ref, w_ref, cb_ref, lg_ref, lb_ref, o_ref, hist_out_ref, scr, rows):
    scr[HC_PAD:, :] = _glu(ca_ref[...], cg_ref[...])
    for r0 in range(0, rows, CONV_CHUNK):
        acc = None
        for t in range(CV_WIDTH):
            start = r0 + HC_PAD - (CV_WIDTH - 1) + t
            term = scr[start:start + CONV_CHUNK, :] * w_ref[t:t + 1, :]
            acc = term if acc is None else acc + term
        zc = acc + cb_ref[...]
        mu = jnp.mean(zc, axis=-1, keepdims=True)
        cen = zc - mu
        var = jnp.mean(cen * cen, axis=-1, keepdims=True)
        y = cen * lax.rsqrt(var + EPS) * lg_ref[...] + lb_ref[...]
        o_ref[r0:r0 + CONV_CHUNK, :] = (y * jax.nn.sigmoid(y)).astype(o_ref.dtype)
    hist_out_ref[0] = scr[rows:rows + HC_PAD, :]


def _mixc_prompt_kernel(ca_ref, cg_ref, cap_ref, cgp_ref, w_ref, cb_ref, lg_ref, lb_ref,
                        o_ref, hist_out_ref, scr):
    first = pl.program_id(0) % (SEQ // MIX_ROWS) == 0
    scr[0:HC_PAD, :] = jnp.where(first, 0.0, _glu(cap_ref[...], cgp_ref[...]))
    _mixc_body(ca_ref, cg_ref, w_ref, cb_ref, lg_ref, lb_ref, o_ref, hist_out_ref, scr, MIX_ROWS)


def _mixc_sample_kernel(ca_ref, cg_ref, hist_ref, w_ref, cb_ref, lg_ref, lb_ref, y_hbm,
                        o_ref, hist_out_ref, scr):
    del y_hbm
    scr[0:HC_PAD, :] = hist_ref[0]
    _mixc_body(ca_ref, cg_ref, w_ref, cb_ref, lg_ref, lb_ref, o_ref, hist_out_ref, scr, DEC_SEQ)


def _row_spec(n):
    return pl.BlockSpec((n, WIDTH_CV), lambda i: (0, 0))


def _mixc_prompt(z_rest, w, cb, lg, lb):
    tiles_per_seq = SEQ // MIX_ROWS
    cur = lambda c: pl.BlockSpec((MIX_ROWS, WIDTH_CV), lambda i: (i, c))
    prev = lambda c: pl.BlockSpec((HC_PAD, WIDTH_CV),
                                  lambda i: (jnp.maximum(i * (MIX_ROWS // HC_PAD) - 1, 0), c))
    return pl.pallas_call(
        _mixc_prompt_kernel,
        grid=(ROWS_P // MIX_ROWS,),
        in_specs=[cur(ZR_CA), cur(ZR_CG), prev(ZR_CA), prev(ZR_CG),
                  _row_spec(CV_WIDTH), _row_spec(1), _row_spec(1), _row_spec(1)],
        out_specs=[pl.BlockSpec((MIX_ROWS, WIDTH_CV), lambda i: (i, 0)),
                   pl.BlockSpec((1, HC_PAD, WIDTH_CV), lambda i: (i // tiles_per_seq, 0, 0))],
        out_shape=[jax.ShapeDtypeStruct((ROWS, WIDTH_CV), BF16),
                   jax.ShapeDtypeStruct((BATCH, HC_PAD, WIDTH_CV), F32)],
        scratch_shapes=[pltpu.VMEM((HC_PAD + MIX_ROWS, WIDTH_CV), F32)],
        compiler_params=_params(("arbitrary",)),
        name="mixc_prompt",
    )(z_rest, z_rest, z_rest, z_rest, w, cb, lg, lb)


def _mixc_sample(z_rest, hist, w, cb, lg, lb, y_cv):
    blk0 = ROWS_P // DEC_SEQ
    cur = lambda c: pl.BlockSpec((DEC_SEQ, WIDTH_CV), lambda b: (b + blk0, c))
    return pl.pallas_call(
        _mixc_sample_kernel,
        grid=(DEC_BATCH,),
        in_specs=[cur(ZR_CA), cur(ZR_CG),
                  pl.BlockSpec((1, HC_PAD, WIDTH_CV), lambda b: (b, 0, 0)),
                  _row_spec(CV_WIDTH), _row_spec(1), _row_spec(1), _row_spec(1),
                  pl.BlockSpec(memory_space=pl.ANY)],
        out_specs=[pl.BlockSpec((DEC_SEQ, WIDTH_CV), lambda b: (b + blk0, 0)),
                   pl.BlockSpec((1, HC_PAD, WIDTH_CV), lambda b: (b, 0, 0))],
        out_shape=[jax.ShapeDtypeStruct((ROWS, WIDTH_CV), BF16),
                   jax.ShapeDtypeStruct((DEC_BATCH, HC_PAD, WIDTH_CV), F32)],
        scratch_shapes=[pltpu.VMEM((HC_PAD + DEC_SEQ, WIDTH_CV), F32)],
        input_output_aliases={7: 0},
        compiler_params=_params(("arbitrary",)),
        name="mixc_sample",
    )(z_rest, z_rest, hist, w, cb, lg, lb, y_cv)


def _outproj_kernel(ya_ref, yb_ref, yc_ref, wa_ref, wb_ref, wc_ref, x_ref, o_ref):
    acc = jnp.dot(ya_ref[...], wa_ref[...], preferred_element_type=F32)
    acc += jnp.dot(yb_ref[...], wb_ref[...], preferred_element_type=F32)
    acc += jnp.dot(yc_ref[...], wc_ref[...], preferred_element_type=F32)
    o_ref[...] = x_ref[...] + acc


def _outproj(y_att, y_sc, y_cv, w_out_b, x):
    row = lambda width: pl.BlockSpec((TM, width), lambda j, i: (i, 0))
    wspec = lambda width, blk: pl.BlockSpec((width, TN), lambda j, i: (blk, j))
    xo = pl.BlockSpec((TM, TN), lambda j, i: (i, j))
    return pl.pallas_call(
        _outproj_kernel,
        grid=(D_MODEL // TN, ROWS // TM),
        in_specs=[row(WIDTH_ATT), row(WIDTH_SC), row(WIDTH_CV),
                  wspec(WIDTH_ATT, 0), wspec(WIDTH_SC, WIDTH_ATT // WIDTH_SC),
                  wspec(WIDTH_CV, (WIDTH_ATT + WIDTH_SC) // WIDTH_CV), xo],
        out_specs=xo,
        out_shape=jax.ShapeDtypeStruct((ROWS, D_MODEL), F32),
        compiler_params=_params(("parallel", "parallel")),
        name="outproj",
    )(y_att, y_sc, y_cv, w_out_b, w_out_b, w_out_b, x)


def _ffn_kernel(h_ref, x_ref, wgu_ref, wd_ref, o_ref):
    @pl.when(pl.program_id(1) == 0)
    def _():
        o_ref[...] = x_ref[...]

    gu = jnp.dot(h_ref[...], wgu_ref[0], preferred_element_type=F32)
    g = gu[:, :TF]
    u = gu[:, TF:]
    act = (g * jax.nn.sigmoid(g) * u).astype(BF16)
    o_ref[...] += jnp.dot(act, wd_ref[...], preferred_element_type=F32)


def _ffn(h, x, w_gu, w_down_b):
    return pl.pallas_call(
        _ffn_kernel,
        grid=(ROWS // TM, D_FF // TF),
        in_specs=[pl.BlockSpec((TM, D_MODEL), lambda i, f: (i, 0)),
                  pl.BlockSpec((TM, D_MODEL), lambda i, f: (i, 0), pipeline_mode=pl.Buffered(1)),
                  pl.BlockSpec((1, D_MODEL, 2 * TF), lambda i, f: (f, 0, 0)),
                  pl.BlockSpec((TF, D_MODEL), lambda i, f: (f, 0))],
        out_specs=pl.BlockSpec((TM, D_MODEL), lambda i, f: (i, 0)),
        out_shape=jax.ShapeDtypeStruct((ROWS, D_MODEL), F32),
        compiler_params=_params(("parallel", "arbitrary")),
        name="ffn",
    )(h, x, w_gu, w_down_b)


def _prompt_bias(rel_bias_l):
    r = jnp.arange(ATT_Q)[:, None]
    c = jnp.arange(ATT_KB * ATT_Q)[None, :]
    dist = ATT_PAST + r - c
    bias = rel_bias_l[:, jnp.clip(dist, -MAX_REL, MAX_REL) + MAX_REL]
    q_chunk = r // CHUNK + LEFT_CHUNKS
    k_chunk = c // CHUNK
    band = (k_chunk <= q_chunk) & (k_chunk >= q_chunk - LEFT_CHUNKS)
    return jnp.where(band[None], bias, NEG).astype(F32)


def _sample_bias(rel_bias_l, n_cache):
    dist = jnp.arange(DEC_SEQ)[:, None] + n_cache - jnp.arange(n_cache + DEC_SEQ)[None, :]
    bias = rel_bias_l[:, jnp.clip(dist, -MAX_REL, MAX_REL) + MAX_REL].astype(F32)
    return bias[:, :, :n_cache], bias[:, :, n_cache:]


def _pad_hist(hist, pad_to):
    return jnp.pad(hist, ((0, 0), (pad_to - hist.shape[1], 0), (0, 0)))


def kernel(x_prompt, x_sample, cache_attn_k, cache_attn_v, cache_conv_b, cache_conv_c, norm_mix_g, w_in,
           rel_bias, conv_b_w, conv_c_w, conv_c_b, ln_c_g, ln_c_b, w_out, norm_ffn_g, w_ffn_gate, w_ffn_up,
           w_ffn_down, final_norm_g):
    n_cache = cache_attn_k.shape[2]
    n_keep = min(ATT_PAST, SEQ)
    x = jnp.concatenate([x_prompt.reshape(ROWS_P, D_MODEL), x_sample.reshape(ROWS_S, D_MODEL)], axis=0)
    outs = {k: [] for k in ("pk", "pv", "pb", "pc", "sk", "sv", "sb", "sc")}
    n_f = D_FF // TF
    for l in range(DEPTH):
        w_in_b = w_in[l].astype(BF16)
        w_out_b = w_out[l].astype(BF16)
        w_gu = jnp.concatenate([w_ffn_gate[l].reshape(D_MODEL, n_f, TF), w_ffn_up[l].reshape(D_MODEL, n_f, TF)],
                               axis=-1).transpose(1, 0, 2).astype(BF16)
        w_down_b = w_ffn_down[l].astype(BF16)

        h = _rmsnorm(x, norm_mix_g[l], BF16, 0, ROWS)
        z_kv, kv, z_rest = _inproj(h, w_in_b)

        y_att = _attn_prompt(z_rest, z_kv, _prompt_bias(rel_bias[l]))
        bias_c, bias_n = _sample_bias(rel_bias[l], n_cache)
        y_att = _attn_sample(z_rest, z_kv, cache_attn_k[l].reshape(DEC_BATCH, n_cache, WIDTH_ATT),
                             cache_attn_v[l].reshape(DEC_BATCH, n_cache, WIDTH_ATT), bias_c, bias_n, y_att)

        y_sc, hb_p = _mixb_prompt(z_rest, conv_b_w[l])
        y_sc, hb_s = _mixb_sample(z_rest, _pad_hist(cache_conv_b[l], HB_PAD), conv_b_w[l], y_sc)

        cb, lg, lb = (a[l].reshape(1, WIDTH_CV) for a in (conv_c_b, ln_c_g, ln_c_b))
        y_cv, hc_p = _mixc_prompt(z_rest, conv_c_w[l], cb, lg, lb)
        y_cv, hc_s = _mixc_sample(z_rest, _pad_hist(cache_conv_c[l], HC_PAD), conv_c_w[l], cb, lg, lb, y_cv)

        x = _outproj(y_att, y_sc, y_cv, w_out_b, x)
        h = _rmsnorm(x, norm_ffn_g[l], BF16, 0, ROWS)
        x = _ffn(h, x, w_gu, w_down_b)

        kv_p = kv[:ROWS_P].reshape(BATCH, SEQ, 2, N_HEADS, HEAD_DIM)[:, SEQ - n_keep:]
        kv_s = kv[ROWS_P:].reshape(DEC_BATCH, DEC_SEQ, 2, N_HEADS, HEAD_DIM)
        outs["pk"].append(kv_p[:, :, 0])
        outs["pv"].append(kv_p[:, :, 1])
        outs["sk"].append(kv_s[:, :, 0])
        outs["sv"].append(kv_s[:, :, 1])
        outs["pb"].append(hb_p[:, HB_PAD - (SC_WIDTH - 1):])
        outs["pc"].append(hc_p[:, HC_PAD - (CV_WIDTH - 1):])
        outs["sb"].append(hb_s[:, HB_PAD - (SC_WIDTH - 1):])
        outs["sc"].append(hc_s[:, HC_PAD - (CV_WIDTH - 1):])

    y_prompt = _rmsnorm(x, final_norm_g, F32, 0, ROWS_P).reshape(BATCH, SEQ, D_MODEL)
    y_sample = _rmsnorm(x, final_norm_g, F32, ROWS_P, ROWS_S).reshape(DEC_BATCH, DEC_SEQ, D_MODEL)
    st = {k: jnp.stack(v) for k, v in outs.items()}
    return (y_prompt, y_sample, st["pk"], st["pv"], st["pb"], st["pc"],
            st["sk"], st["sv"], st["sb"], st["sc"])
```

```python
import jax
import jax.numpy as jnp
from jax import lax
from jax.experimental import pallas as pl
from jax.experimental.pallas import tpu as pltpu

F32 = jnp.float32
BF16 = jnp.bfloat16

D_MODEL = 4096
BATCH = 8
SEQ = 2048
DEPTH = 2
DEC_BATCH = 16
DEC_SEQ = 32
CHUNK = 64
LEFT_CHUNKS = 8
ATT_PAST = LEFT_CHUNKS * CHUNK
WIDTH_ATT = 2048
WIDTH_SC = 1024
WIDTH_CV = 1024
HEAD_DIM = 128
N_HEADS = 16
MAX_REL = 256
SC_WIDTH = 3
CV_WIDTH = 31
D_FF = 11008
D_IN = 11264
EPS = 1e-6

ROWS_P = BATCH * SEQ
ROWS_S = DEC_BATCH * DEC_SEQ
ROWS = ROWS_P + ROWS_S
N_KEEP = min(ATT_PAST, SEQ)

TM = 512
TN = 1024
TM_REST = ROWS // 16
TN_REST = 512
TF = 256
NORM_ROWS = 256
ATT_Q = 128
ATT_KB = ATT_PAST // ATT_Q + 1
MIX_ROWS = 256
CONV_CHUNK = 32
HB_PAD = 8
HC_PAD = 32
NEG = -1e30
VMEM_LIMIT = 56 * 1024 * 1024

P_TILES = ROWS_P // TM
TILES_PER_SEQ = SEQ // TM

ZR_SB, ZR_SC, ZR_SH, ZR_CA, ZR_CG = 2, 3, 4, 5, 6

assert TM == N_KEEP == ROWS_S and ROWS == (P_TILES + 1) * TM


def _params(*sem):
    return pltpu.CompilerParams(dimension_semantics=sem, vmem_limit_bytes=VMEM_LIMIT)


_ANY = pl.BlockSpec(memory_space=pl.ANY)


def _rmsnorm_kernel(x_ref, g_ref, *rest):
    o_ref = rest[-1]
    x = x_ref[...]
    ms = jnp.mean(x * x, axis=-1, keepdims=True)
    o_ref[...] = (x * lax.rsqrt(ms + EPS) * g_ref[...]).astype(o_ref.dtype)


def _rmsnorm(x, g, out_dtype, src_row0, rows, out_rows, dst_row0=0, dst=None):
    src0, dst0 = src_row0 // NORM_ROWS, dst_row0 // NORM_ROWS
    args = [x, g.reshape(1, D_MODEL)]
    in_specs = [pl.BlockSpec((NORM_ROWS, D_MODEL), lambda i: (i + src0, 0)),
                pl.BlockSpec((1, D_MODEL), lambda i: (0, 0))]
    aliases = {}
    if dst is not None:
        args.append(dst)
        in_specs.append(_ANY)
        aliases = {2: 0}
    return pl.pallas_call(
        _rmsnorm_kernel,
        grid=(rows // NORM_ROWS,),
        in_specs=in_specs,
        out_specs=pl.BlockSpec((NORM_ROWS, D_MODEL), lambda i: (i + dst0, 0)),
        out_shape=jax.ShapeDtypeStruct((out_rows, D_MODEL), out_dtype),
        input_output_aliases=aliases,
        compiler_params=_params("parallel"),
        name="rmsnorm",
    )(*args)


def _cast_gu_kernel(wg_ref, wu_ref, o_ref):
    o_ref[0, 0, :, :TF] = wg_ref[0].astype(BF16)
    o_ref[0, 0, :, TF:] = wu_ref[0].astype(BF16)


def _cast_gate_up(w_gate, w_up):
    n_f = D_FF // TF
    src = pl.BlockSpec((1, D_MODEL, TF), lambda l, f: (l, 0, f))
    return pl.pallas_call(
        _cast_gu_kernel,
        grid=(DEPTH, n_f),
        in_specs=[src, src],
        out_specs=pl.BlockSpec((1, 1, D_MODEL, 2 * TF), lambda l, f: (l, f, 0, 0)),
        out_shape=jax.ShapeDtypeStruct((DEPTH, n_f, D_MODEL, 2 * TF), BF16),
        compiler_params=_params("parallel", "parallel"),
        name="cast_gate_up",
    )(w_gate, w_up)


def _cast_kernel(w_ref, o_ref):
    o_ref[...] = w_ref[...].astype(BF16)


def _cast_down(w_down):
    blk = pl.BlockSpec((1, TF, D_MODEL), lambda l, f: (l, f, 0))
    return pl.pallas_call(
        _cast_kernel,
        grid=(DEPTH, D_FF // TF),
        in_specs=[blk],
        out_specs=blk,
        out_shape=jax.ShapeDtypeStruct(w_down.shape, BF16),
        compiler_params=_params("parallel", "parallel"),
        name="cast_down",
    )(w_down)


def _stationary_weights(w_ref, wb_ref):
    @pl.when(pl.program_id(1) == 0)
    def _():
        wb_ref[...] = w_ref[0].astype(BF16)


def _inproj_kv_kernel(a_ref, w_ref, *rest):
    z_ref, pc_ref, sc_ref, wb_ref = rest[-4:]
    _stationary_weights(w_ref, wb_ref)
    i = pl.program_id(1)
    r = jnp.dot(a_ref[...], wb_ref[...], preferred_element_type=F32)
    z_ref[...] = r.astype(BF16)

    @pl.when((i < P_TILES) & (i % TILES_PER_SEQ == TILES_PER_SEQ - 1))
    def _():
        pc_ref[0, 0] = r

    @pl.when(i == P_TILES)
    def _():
        sc_ref[0] = r


def _inproj_kv(h, w_in, layer, col0, p_cache, s_cache):
    blk0 = col0 // TN
    a_spec = pl.BlockSpec((TM, D_MODEL), lambda j, i: (i, 0))
    w_spec = pl.BlockSpec((1, D_MODEL, TN), lambda j, i: (layer, 0, j + blk0), pipeline_mode=pl.Buffered(1))
    in_specs, args, aliases = [a_spec, w_spec], [h, w_in], {}
    if p_cache is not None:
        in_specs += [_ANY, _ANY]
        args += [p_cache, s_cache]
        aliases = {2: 1, 3: 2}

    def p_map(j, i):
        return (layer, jnp.minimum(i, P_TILES - 1) // TILES_PER_SEQ, 0, j)

    return pl.pallas_call(
        _inproj_kv_kernel,
        grid=(WIDTH_ATT // TN, ROWS // TM),
        in_specs=in_specs,
        out_specs=[pl.BlockSpec((TM, TN), lambda j, i: (i, j)),
                   pl.BlockSpec((1, 1, N_KEEP, TN), p_map),
                   pl.BlockSpec((1, ROWS_S, TN), lambda j, i: (layer, 0, j))],
        out_shape=[jax.ShapeDtypeStruct((ROWS, WIDTH_ATT), BF16),
                   jax.ShapeDtypeStruct((DEPTH, BATCH, N_KEEP, WIDTH_ATT), F32),
                   jax.ShapeDtypeStruct((DEPTH, ROWS_S, WIDTH_ATT), F32)],
        scratch_shapes=[pltpu.VMEM((D_MODEL, TN), BF16)],
        input_output_aliases=aliases,
        compiler_params=_params("arbitrary", "arbitrary"),
        name="inproj_kv",
    )(*args)


def _inproj_rest_kernel(a_ref, w_ref, z_ref, wb_ref):
    _stationary_weights(w_ref, wb_ref)
    z_ref[...] = jnp.dot(a_ref[...], wb_ref[...], preferred_element_type=F32).astype(BF16)


def _inproj_rest(h, w_in, layer):
    n_q = WIDTH_ATT // TN_REST
    n_skip = 2 * WIDTH_ATT // TN_REST
    width = D_IN - 2 * WIDTH_ATT
    return pl.pallas_call(
        _inproj_rest_kernel,
        grid=(width // TN_REST, ROWS // TM_REST),
        in_specs=[pl.BlockSpec((TM_REST, D_MODEL), lambda j, i: (i, 0)),
                  pl.BlockSpec((1, D_MODEL, TN_REST),
                               lambda j, i: (layer, 0, jnp.where(j < n_q, j, j + n_skip)))],
        out_specs=pl.BlockSpec((TM_REST, TN_REST), lambda j, i: (i, j)),
        out_shape=jax.ShapeDtypeStruct((ROWS, width), BF16),
        scratch_shapes=[pltpu.VMEM((D_MODEL, TN_REST), BF16)],
        compiler_params=_params("arbitrary", "arbitrary"),
        name="inproj_rest",
    )(h, w_in)


def _softmax_pv(parts, vals):
    m = parts[0].max(axis=-1, keepdims=True)
    for s in parts[1:]:
        m = jnp.maximum(m, s.max(axis=-1, keepdims=True))
    acc = None
    l = None
    for s, v in zip(parts, vals):
        p = jnp.exp(s - m)
        ls = p.sum(axis=-1, keepdims=True)
        o = jnp.dot(p.astype(BF16), v, preferred_element_type=F32)
        acc = o if acc is None else acc + o
        l = ls if l is None else l + ls
    return acc / l


def _qk(q, k):
    return lax.dot_general(q, k, (((1,), (1,)), ((), ())), preferred_element_type=F32)


def _attn_prompt_kernel(q_ref, *refs):
    k_refs = refs[:ATT_KB]
    v_refs = refs[ATT_KB:2 * ATT_KB]
    bias_ref = refs[2 * ATT_KB]
    o_ref = refs[2 * ATT_KB + 1]
    blk = pl.program_id(0) % (SEQ // ATT_Q)
    col = lax.broadcasted_iota(jnp.int32, (1, ATT_KB * ATT_Q), 1)
    pen = jnp.where(col >= (ATT_KB - 1 - blk) * ATT_Q, 0.0, NEG).astype(F32)
    scale = HEAD_DIM ** -0.5
    for h in range(N_HEADS):
        hs = slice(h * HEAD_DIM, (h + 1) * HEAD_DIM)
        k_all = jnp.concatenate([r[:, hs] for r in k_refs], axis=0)
        v_all = jnp.concatenate([r[:, hs] for r in v_refs], axis=0)
        s = _qk(q_ref[:, hs], k_all) * scale + bias_ref[h] + pen
        o_ref[:, hs] = _softmax_pv([s], [v_all]).astype(o_ref.dtype)


def _attn_prompt(z_rest, z_k, z_v, bias):
    blocks_per_seq = SEQ // ATT_Q

    def kv_map(d):
        def index(n):
            first = (n // blocks_per_seq) * blocks_per_seq
            return (jnp.maximum(n - (ATT_KB - 1) + d, first), 0)
        return index

    blk = (ATT_Q, WIDTH_ATT)
    in_specs = [pl.BlockSpec(blk, lambda n: (n, 0))]
    in_specs += [pl.BlockSpec(blk, kv_map(d)) for d in range(ATT_KB)]
    in_specs += [pl.BlockSpec(blk, kv_map(d)) for d in range(ATT_KB)]
    in_specs += [pl.BlockSpec((N_HEADS, ATT_Q, ATT_KB * ATT_Q), lambda n: (0, 0, 0))]
    return pl.pallas_call(
        _attn_prompt_kernel,
        grid=(ROWS_P // ATT_Q,),
        in_specs=in_specs,
        out_specs=pl.BlockSpec(blk, lambda n: (n, 0)),
        out_shape=jax.ShapeDtypeStruct((ROWS, WIDTH_ATT), BF16),
        compiler_params=_params("parallel"),
        name="attn_prompt",
    )(z_rest, *([z_k] * ATT_KB), *([z_v] * ATT_KB), bias)


def _attn_sample_kernel(q_ref, kn_ref, vn_ref, ck_ref, cv_ref, bias_c_ref, bias_n_ref, y_hbm, o_ref):
    del y_hbm
    scale = HEAD_DIM ** -0.5
    for h in range(N_HEADS):
        hs = slice(h * HEAD_DIM, (h + 1) * HEAD_DIM)
        q = q_ref[:, hs]
        s_c = _qk(q, ck_ref[0, 0, :, hs].astype(BF16)) * scale + bias_c_ref[h]
        s_n = _qk(q, kn_ref[:, hs]) * scale + bias_n_ref[h]
        o = _softmax_pv([s_c, s_n], [cv_ref[0, 0, :, hs].astype(BF16), vn_ref[:, hs]])
        o_ref[:, hs] = o.astype(o_ref.dtype)


def _attn_sample(z_rest, z_k, z_v, cache_k, cache_v, layer, bias_c, bias_n, y_att):
    blk0 = ROWS_P // DEC_SEQ
    n_cache = cache_k.shape[2]
    blk = (DEC_SEQ, WIDTH_ATT)
    rows = pl.BlockSpec(blk, lambda b: (b + blk0, 0))
    cache_spec = pl.BlockSpec((1, 1, n_cache, WIDTH_ATT), lambda b: (layer, b, 0, 0))
    return pl.pallas_call(
        _attn_sample_kernel,
        grid=(DEC_BATCH,),
        in_specs=[rows, rows, rows, cache_spec, cache_spec,
                  pl.BlockSpec((N_HEADS, DEC_SEQ, n_cache), lambda b: (0, 0, 0)),
                  pl.BlockSpec((N_HEADS, DEC_SEQ, DEC_SEQ), lambda b: (0, 0, 0)),
                  _ANY],
        out_specs=rows,
        out_shape=jax.ShapeDtypeStruct((ROWS, WIDTH_ATT), BF16),
        input_output_aliases={7: 0},
        compiler_params=_params("parallel"),
        name="attn_sample",
    )(z_rest, z_k, z_v, cache_k, cache_v, bias_c, bias_n, y_att)


def _mixb_body(sb_ref, sc_ref, sh_ref, w_ref, o_ref, hist_out_ref, scr, rows):
    scr[HB_PAD:, :] = sc_ref[...].astype(F32) * sh_ref[...].astype(F32)
    for r0 in range(0, rows, CONV_CHUNK):
        acc = None
        for t in range(SC_WIDTH):
            start = r0 + HB_PAD - (SC_WIDTH - 1) + t
            term = scr[start:start + CONV_CHUNK, :] * w_ref[0, t:t + 1, :]
            acc = term if acc is None else acc + term
        y = sb_ref[r0:r0 + CONV_CHUNK, :].astype(F32) * acc
        o_ref[r0:r0 + CONV_CHUNK, :] = y.astype(o_ref.dtype)
    hist_out_ref[0] = scr[rows:rows + HB_PAD, :]


def _mixb_prompt_kernel(sb_ref, sc_ref, sh_ref, scp_ref, shp_ref, w_ref, o_ref, hist_out_ref, scr):
    first = pl.program_id(0) % (SEQ // MIX_ROWS) == 0
    prev = scp_ref[...].astype(F32) * shp_ref[...].astype(F32)
    prev = jnp.where(first, 0.0, prev)
    scr[0:HB_PAD, :] = prev[prev.shape[0] - HB_PAD:, :]
    _mixb_body(sb_ref, sc_ref, sh_ref, w_ref, o_ref, hist_out_ref, scr, MIX_ROWS)


def _mixb_sample_kernel(sb_ref, sc_ref, sh_ref, hist_ref, w_ref, y_hbm, o_ref, hist_out_ref, scr):
    del y_hbm
    scr[0:HB_PAD, :] = hist_ref[0]
    _mixb_body(sb_ref, sc_ref, sh_ref, w_ref, o_ref, hist_out_ref, scr, DEC_SEQ)


def _layer_rows(n, width, layer):
    return pl.BlockSpec((1, n, width), lambda i: (layer, 0, 0))


def _mixb_prompt(z_rest, w, layer):
    halo = 16
    tiles_per_seq = SEQ // MIX_ROWS
    cur = lambda c: pl.BlockSpec((MIX_ROWS, WIDTH_SC), lambda i: (i, c))
    prev = lambda c: pl.BlockSpec((halo, WIDTH_SC),
                                  lambda i: (jnp.maximum(i * (MIX_ROWS // halo) - 1, 0), c))
    return pl.pallas_call(
        _mixb_prompt_kernel,
        grid=(ROWS_P // MIX_ROWS,),
        in_specs=[cur(ZR_SB), cur(ZR_SC), cur(ZR_SH), prev(ZR_SC), prev(ZR_SH),
                  _layer_rows(SC_WIDTH, WIDTH_SC, layer)],
        out_specs=[pl.BlockSpec((MIX_ROWS, WIDTH_SC), lambda i: (i, 0)),
                   pl.BlockSpec((1, HB_PAD, WIDTH_SC), lambda i: (i // tiles_per_seq, 0, 0))],
        out_shape=[jax.ShapeDtypeStruct((ROWS, WIDTH_SC), BF16),
                   jax.ShapeDtypeStruct((BATCH, HB_PAD, WIDTH_SC), F32)],
        scratch_shapes=[pltpu.VMEM((HB_PAD + MIX_ROWS, WIDTH_SC), F32)],
        compiler_params=_params("arbitrary"),
        name="mixb_prompt",
    )(z_rest, z_rest, z_rest, z_rest, z_rest, w)


def _mixb_sample(z_rest, hist, w, layer, y_sc):
    blk0 = ROWS_P // DEC_SEQ
    cur = lambda c: pl.BlockSpec((DEC_SEQ, WIDTH_SC), lambda b: (b + blk0, c))
    return pl.pallas_call(
        _mixb_sample_kernel,
        grid=(DEC_BATCH,),
        in_specs=[cur(ZR_SB), cur(ZR_SC), cur(ZR_SH),
                  pl.BlockSpec((1, HB_PAD, WIDTH_SC), lambda b: (b, 0, 0)),
                  _layer_rows(SC_WIDTH, WIDTH_SC, layer),
                  _ANY],
        out_specs=[pl.BlockSpec((DEC_SEQ, WIDTH_SC), lambda b: (b + blk0, 0)),
                   pl.BlockSpec((1, HB_PAD, WIDTH_SC), lambda b: (b, 0, 0))],
        out_shape=[jax.ShapeDtypeStruct((ROWS, WIDTH_SC), BF16),
                   jax.ShapeDtypeStruct((DEC_BATCH, HB_PAD, WIDTH_SC), F32)],
        scratch_shapes=[pltpu.VMEM((HB_PAD + DEC_SEQ, WIDTH_SC), F32)],
        input_output_aliases={5: 0},
        compiler_params=_params("arbitrary"),
        name="mixb_sample",
    )(z_rest, z_rest, z_rest, hist, w, y_sc)


def _glu(a, g):
    return a.astype(F32) * jax.nn.sigmoid(g.astype(F32))


def _mixc_body(ca_ref, cg_ref, w_ref, cb_ref, lg_ref, lb_ref, o_ref, hist_out_ref, scr, rows):
    scr[HC_PAD:, :] = _glu(ca_ref[...], cg_ref[...])
    for r0 in range(0, rows, CONV_CHUNK):
        acc = None
        for t in range(CV_WIDTH):
            start = r0 + HC_PAD - (CV_WIDTH - 1) + t
            term = scr[start:start + CONV_CHUNK, :] * w_ref[0, t:t + 1, :]
            acc = term if acc is None else acc + term
        zc = acc + cb_ref[0]
        mu = jnp.mean(zc, axis=-1, keepdims=True)
        cen = zc - mu
        var = jnp.mean(cen * cen, axis=-1, keepdims=True)
        y = cen * lax.rsqrt(var + EPS) * lg_ref[0] + lb_ref[0]
        o_ref[r0:r0 + CONV_CHUNK, :] = (y * jax.nn.sigmoid(y)).astype(o_ref.dtype)
    hist_out_ref[0] = scr[rows:rows + HC_PAD, :]


def _mixc_prompt_kernel(ca_ref, cg_ref, cap_ref, cgp_ref, w_ref, cb_ref, lg_ref, lb_ref,
                        o_ref, hist_out_ref, scr):
    first = pl.program_id(0) % (SEQ // MIX_ROWS) == 0
    scr[0:HC_PAD, :] = jnp.where(first, 0.0, _glu(cap_ref[...], cgp_ref[...]))
    _mixc_body(ca_ref, cg_ref, w_ref, cb_ref, lg_ref, lb_ref, o_ref, hist_out_ref, scr, MIX_ROWS)


def _mixc_sample_kernel(ca_ref, cg_ref, hist_ref, w_ref, cb_ref, lg_ref, lb_ref, y_hbm,
                        o_ref, hist_out_ref, scr):
    del y_hbm
    scr[0:HC_PAD, :] = hist_ref[0]
    _mixc_body(ca_ref, cg_ref, w_ref, cb_ref, lg_ref, lb_ref, o_ref, hist_out_ref, scr, DEC_SEQ)


def _mixc_prompt(z_rest, w, cb, lg, lb, layer):
    tiles_per_seq = SEQ // MIX_ROWS
    cur = lambda c: pl.BlockSpec((MIX_ROWS, WIDTH_CV), lambda i: (i, c))
    prev = lambda c: pl.BlockSpec((HC_PAD, WIDTH_CV),
                                  lambda i: (jnp.maximum(i * (MIX_ROWS // HC_PAD) - 1, 0), c))
    vec = _layer_rows(1, WIDTH_CV, layer)
    return pl.pallas_call(
        _mixc_prompt_kernel,
        grid=(ROWS_P // MIX_ROWS,),
        in_specs=[cur(ZR_CA), cur(ZR_CG), prev(ZR_CA), prev(ZR_CG),
                  _layer_rows(CV_WIDTH, WIDTH_CV, layer), vec, vec, vec],
        out_specs=[pl.BlockSpec((MIX_ROWS, WIDTH_CV), lambda i: (i, 0)),
                   pl.BlockSpec((1, HC_PAD, WIDTH_CV), lambda i: (i // tiles_per_seq, 0, 0))],
        out_shape=[jax.ShapeDtypeStruct((ROWS, WIDTH_CV), BF16),
                   jax.ShapeDtypeStruct((BATCH, HC_PAD, WIDTH_CV), F32)],
        scratch_shapes=[pltpu.VMEM((HC_PAD + MIX_ROWS, WIDTH_CV), F32)],
        compiler_params=_params("arbitrary"),
        name="mixc_prompt",
    )(z_rest, z_rest, z_rest, z_rest, w, cb, lg, lb)


def _mixc_sample(z_rest, hist, w, cb, lg, lb, layer, y_cv):
    blk0 = ROWS_P // DEC_SEQ
    cur = lambda c: pl.BlockSpec((DEC_SEQ, WIDTH_CV), lambda b: (b + blk0, c))
    vec = _layer_rows(1, WIDTH_CV, layer)
    return pl.pallas_call(
        _mixc_sample_kernel,
        grid=(DEC_BATCH,),
        in_specs=[cur(ZR_CA), cur(ZR_CG),
                  pl.BlockSpec((1, HC_PAD, WIDTH_CV), lambda b: (b, 0, 0)),
                  _layer_rows(CV_WIDTH, WIDTH_CV, layer), vec, vec, vec,
                  _ANY],
        out_specs=[pl.BlockSpec((DEC_SEQ, WIDTH_CV), lambda b: (b + blk0, 0)),
                   pl.BlockSpec((1, HC_PAD, WIDTH_CV), lambda b: (b, 0, 0))],
        out_shape=[jax.ShapeDtypeStruct((ROWS, WIDTH_CV), BF16),
                   jax.ShapeDtypeStruct((DEC_BATCH, HC_PAD, WIDTH_CV), F32)],
        scratch_shapes=[pltpu.VMEM((HC_PAD + DEC_SEQ, WIDTH_CV), F32)],
        input_output_aliases={7: 0},
        compiler_params=_params("arbitrary"),
        name="mixc_sample",
    )(z_rest, z_rest, hist, w, cb, lg, lb, y_cv)


def _outproj_kernel(ya_ref, yb_ref, yc_ref, w_ref, xp_ref, xs_ref, o_ref, wb_ref):
    _stationary_weights(w_ref, wb_ref)
    a0, a1 = WIDTH_ATT, WIDTH_ATT + WIDTH_SC
    acc = jnp.dot(ya_ref[...], wb_ref[:a0, :], preferred_element_type=F32)
    acc += jnp.dot(yb_ref[...], wb_ref[a0:a1, :], preferred_element_type=F32)
    acc += jnp.dot(yc_ref[...], wb_ref[a1:, :], preferred_element_type=F32)
    x = jnp.where(pl.program_id(1) < P_TILES, xp_ref[...], xs_ref[...])
    o_ref[...] = x + acc


def _outproj(y_att, y_sc, y_cv, w_out, layer, x_p, x_s, s_blk):
    row = lambda width: pl.BlockSpec((TM, width), lambda j, i: (i, 0))
    return pl.pallas_call(
        _outproj_kernel,
        grid=(D_MODEL // TN, ROWS // TM),
        in_specs=[row(WIDTH_ATT), row(WIDTH_SC), row(WIDTH_CV),
                  pl.BlockSpec((1, D_MODEL, TN), lambda j, i: (layer, 0, j), pipeline_mode=pl.Buffered(1)),
                  pl.BlockSpec((TM, TN), lambda j, i: (jnp.minimum(i, P_TILES - 1), j)),
                  pl.BlockSpec((TM, TN), lambda j, i: (s_blk, j))],
        out_specs=pl.BlockSpec((TM, TN), lambda j, i: (i, j)),
        out_shape=jax.ShapeDtypeStruct((ROWS, D_MODEL), F32),
        scratch_shapes=[pltpu.VMEM((D_MODEL, TN), BF16)],
        compiler_params=_params("arbitrary", "arbitrary"),
        name="outproj",
    )(y_att, y_sc, y_cv, w_out, x_p, x_s)


def _ffn_kernel(h_ref, x_ref, wgu_ref, wd_ref, o_ref):
    @pl.when(pl.program_id(1) == 0)
    def _():
        o_ref[...] = x_ref[...]

    gu = jnp.dot(h_ref[...], wgu_ref[0, 0], preferred_element_type=F32)
    g = gu[:, :TF]
    u = gu[:, TF:]
    act = (g * jax.nn.sigmoid(g) * u).astype(BF16)
    o_ref[...] += jnp.dot(act, wd_ref[0], preferred_element_type=F32)


def _ffn(h, x, w_gu, w_down_b, layer):
    return pl.pallas_call(
        _ffn_kernel,
        grid=(ROWS // TM, D_FF // TF),
        in_specs=[pl.BlockSpec((TM, D_MODEL), lambda i, f: (i, 0)),
                  pl.BlockSpec((TM, D_MODEL), lambda i, f: (i, 0), pipeline_mode=pl.Buffered(1)),
                  pl.BlockSpec((1, 1, D_MODEL, 2 * TF), lambda i, f: (layer, f, 0, 0)),
                  pl.BlockSpec((1, TF, D_MODEL), lambda i, f: (layer, f, 0))],
        out_specs=pl.BlockSpec((TM, D_MODEL), lambda i, f: (i, 0)),
        out_shape=jax.ShapeDtypeStruct((ROWS, D_MODEL), F32),
        compiler_params=_params("parallel", "arbitrary"),
        name="ffn",
    )(h, x, w_gu, w_down_b)


def _hankel(u, rows, cols):
    n_heads = u.shape[0]
    period = rows + cols
    u = jnp.pad(u, ((0, 0), (0, period - 1 - u.shape[1])))
    flat = jnp.broadcast_to(u[:, None, :], (n_heads, rows, period - 1)).reshape(n_heads, rows * (period - 1))
    flat = jnp.pad(flat, ((0, 0), (0, rows)))
    return flat.reshape(n_heads, rows, period)[:, :, :cols]


def _rel_bias_tile(rel_bias_l, rows, cols, n_past):
    t = jnp.arange(rows + cols - 1)
    dist = t + n_past - (cols - 1)
    u = rel_bias_l[:, jnp.clip(dist, -MAX_REL, MAX_REL) + MAX_REL]
    return _hankel(u, rows, cols)[:, :, ::-1].astype(F32)


def _prompt_bias(rel_bias_l):
    bias = _rel_bias_tile(rel_bias_l, ATT_Q, ATT_KB * ATT_Q, ATT_PAST)
    r = jnp.arange(ATT_Q)[:, None]
    c = jnp.arange(ATT_KB * ATT_Q)[None, :]
    q_chunk = r // CHUNK + LEFT_CHUNKS
    k_chunk = c // CHUNK
    band = (k_chunk <= q_chunk) & (k_chunk >= q_chunk - LEFT_CHUNKS)
    return jnp.where(band[None], bias, NEG)


def _pad_hist(hist, pad_to):
    return jnp.pad(hist, ((0, 0), (pad_to - hist.shape[1], 0), (0, 0)))


def kernel(x_prompt, x_sample, cache_attn_k, cache_attn_v, cache_conv_b, cache_conv_c, norm_mix_g, w_in,
           rel_bias, conv_b_w, conv_c_w, conv_c_b, ln_c_g, ln_c_b, w_out, norm_ffn_g, w_ffn_gate, w_ffn_up,
           w_ffn_down, final_norm_g):
    n_cache = cache_attn_k.shape[2]
    x_p = x_prompt.reshape(ROWS_P, D_MODEL)
    x_s = x_sample.reshape(ROWS_S, D_MODEL)
    cache_k = cache_attn_k.reshape(DEPTH, DEC_BATCH, n_cache, WIDTH_ATT)
    cache_v = cache_attn_v.reshape(DEPTH, DEC_BATCH, n_cache, WIDTH_ATT)
    cb, lg, lb = (a.reshape(DEPTH, 1, WIDTH_CV) for a in (conv_c_b, ln_c_g, ln_c_b))
    w_gu = _cast_gate_up(w_ffn_gate, w_ffn_up)
    w_down_b = _cast_down(w_ffn_down)

    hists = {k: [] for k in ("pb", "pc", "sb", "sc")}
    pk = pv = sk = sv = None
    x = None
    for l in range(DEPTH):
        if l == 0:
            h = _rmsnorm(x_p, norm_mix_g[l], BF16, 0, ROWS_P, ROWS)
            h = _rmsnorm(x_s, norm_mix_g[l], BF16, 0, ROWS_S, ROWS, dst_row0=ROWS_P, dst=h)
        else:
            h = _rmsnorm(x, norm_mix_g[l], BF16, 0, ROWS, ROWS)
        z_k, pk, sk = _inproj_kv(h, w_in, l, WIDTH_ATT, pk, sk)
        z_v, pv, sv = _inproj_kv(h, w_in, l, 2 * WIDTH_ATT, pv, sv)
        z_rest = _inproj_rest(h, w_in, l)

        y_att = _attn_prompt(z_rest, z_k, z_v, _prompt_bias(rel_bias[l]))
        bias_s = _rel_bias_tile(rel_bias[l], DEC_SEQ, n_cache + DEC_SEQ, n_cache)
        y_att = _attn_sample(z_rest, z_k, z_v, cache_k, cache_v, l,
                             bias_s[:, :, :n_cache], bias_s[:, :, n_cache:], y_att)

        y_sc, hb_p = _mixb_prompt(z_rest, conv_b_w, l)
        y_sc, hb_s = _mixb_sample(z_rest, _pad_hist(cache_conv_b[l], HB_PAD), conv_b_w, l, y_sc)
        y_cv, hc_p = _mixc_prompt(z_rest, conv_c_w, cb, lg, lb, l)
        y_cv, hc_s = _mixc_sample(z_rest, _pad_hist(cache_conv_c[l], HC_PAD), conv_c_w, cb, lg, lb, l, y_cv)

        if l == 0:
            x = _outproj(y_att, y_sc, y_cv, w_out, l, x_p, x_s, 0)
        else:
            x = _outproj(y_att, y_sc, y_cv, w_out, l, x, x, P_TILES)
        h = _rmsnorm(x, norm_ffn_g[l], BF16, 0, ROWS, ROWS)
        x = _ffn(h, x, w_gu, w_down_b, l)

        hists["pb"].append(hb_p[:, HB_PAD - (SC_WIDTH - 1):])
        hists["pc"].append(hc_p[:, HC_PAD - (CV_WIDTH - 1):])
        hists["sb"].append(hb_s[:, HB_PAD - (SC_WIDTH - 1):])
        hists["sc"].append(hc_s[:, HC_PAD - (CV_WIDTH - 1):])

    y_prompt = _rmsnorm(x, final_norm_g, F32, 0, ROWS_P, ROWS_P).reshape(BATCH, SEQ, D_MODEL)
    y_sample = _rmsnorm(x, final_norm_g, F32, ROWS_P, ROWS_S, ROWS_S).reshape(DEC_BATCH, DEC_SEQ, D_MODEL)
    st = {k: jnp.stack(v) for k, v in hists.items()}
    heads_p = (DEPTH, BATCH, N_KEEP, N_HEADS, HEAD_DIM)
    heads_s = (DEPTH, DEC_BATCH, DEC_SEQ, N_HEADS, HEAD_DIM)
    return (y_prompt, y_sample, pk.reshape(heads_p), pv.reshape(heads_p), st["pb"], st["pc"],
            sk.reshape(heads_s), sv.reshape(heads_s), st["sb"], st["sc"])
```

```python
import jax
import jax.numpy as jnp
from jax import lax
from jax.experimental import pallas as pl
from jax.experimental.pallas import tpu as pltpu

F32 = jnp.float32
BF16 = jnp.bfloat16

D_MODEL = 4096
BATCH = 8
SEQ = 2048
DEPTH = 2
DEC_BATCH = 16
DEC_SEQ = 32
CHUNK = 64
LEFT_CHUNKS = 8
ATT_PAST = LEFT_CHUNKS * CHUNK
WIDTH_ATT = 2048
WIDTH_SC = 1024
WIDTH_CV = 1024
HEAD_DIM = 128
N_HEADS = 16
MAX_REL = 256
SC_WIDTH = 3
CV_WIDTH = 31
D_FF = 11008
D_IN = 11264
EPS = 1e-6

ROWS_P = BATCH * SEQ
ROWS_S = DEC_BATCH * DEC_SEQ
ROWS = ROWS_P + ROWS_S
N_KEEP = min(ATT_PAST, SEQ)

TM = 512
TN = 1024
TM_REST = ROWS // 16
TN_REST = 512
TF = 256
NORM_ROWS = 256
NORM_CHUNK = 16
NORM_COLS = 512
ATT_Q = 256
ATT_KB = ATT_PAST // ATT_Q + 1
MIX_ROWS = 256
CONV_CHUNK = 32
HB_PAD = 8
HC_PAD = 32
NEG = -1e30
LANES = 128
SUBLANES = 8
VMEM_LIMIT = 56 * 1024 * 1024

P_TILES = ROWS_P // TM
TILES_PER_SEQ = SEQ // TM

ZR_SB, ZR_SC, ZR_SH, ZR_CA, ZR_CG = 2, 3, 4, 5, 6

assert TM == N_KEEP == ROWS_S and ROWS == (P_TILES + 1) * TM


def _params(*sem):
    return pltpu.CompilerParams(dimension_semantics=sem, vmem_limit_bytes=VMEM_LIMIT)


_ANY = pl.BlockSpec(memory_space=pl.ANY)


def _rmsnorm_kernel(x_ref, g_ref, *rest):
    o_ref = rest[-1]
    x = x_ref[...]
    ms = jnp.mean(x * x, axis=-1, keepdims=True)
    o_ref[...] = (x * lax.rsqrt(ms + EPS) * g_ref[...]).astype(o_ref.dtype)


def _rmsnorm(x, g, out_dtype, src_row0, rows, out_rows, dst_row0=0, dst=None):
    src0, dst0 = src_row0 // NORM_ROWS, dst_row0 // NORM_ROWS
    args = [x, g.reshape(1, D_MODEL)]
    in_specs = [pl.BlockSpec((NORM_ROWS, D_MODEL), lambda i: (i + src0, 0)),
                pl.BlockSpec((1, D_MODEL), lambda i: (0, 0))]
    aliases = {}
    if dst is not None:
        args.append(dst)
        in_specs.append(_ANY)
        aliases = {2: 0}
    return pl.pallas_call(
        _rmsnorm_kernel,
        grid=(rows // NORM_ROWS,),
        in_specs=in_specs,
        out_specs=pl.BlockSpec((NORM_ROWS, D_MODEL), lambda i: (i + dst0, 0)),
        out_shape=jax.ShapeDtypeStruct((out_rows, D_MODEL), out_dtype),
        input_output_aliases=aliases,
        compiler_params=_params("parallel"),
        name="rmsnorm",
    )(*args)


def _cast_gu_kernel(wg_ref, wu_ref, o_ref):
    o_ref[0, 0, :, :TF] = wg_ref[0].astype(BF16)
    o_ref[0, 0, :, TF:] = wu_ref[0].astype(BF16)


def _cast_gate_up(w_gate, w_up):
    n_f = D_FF // TF
    src = pl.BlockSpec((1, D_MODEL, TF), lambda l, f: (l, 0, f))
    return pl.pallas_call(
        _cast_gu_kernel,
        grid=(DEPTH, n_f),
        in_specs=[src, src],
        out_specs=pl.BlockSpec((1, 1, D_MODEL, 2 * TF), lambda l, f: (l, f, 0, 0)),
        out_shape=jax.ShapeDtypeStruct((DEPTH, n_f, D_MODEL, 2 * TF), BF16),
        compiler_params=_params("parallel", "parallel"),
        name="cast_gate_up",
    )(w_gate, w_up)


def _cast_kernel(w_ref, o_ref):
    o_ref[...] = w_ref[...].astype(BF16)


def _cast_down(w_down):
    blk = pl.BlockSpec((1, TF, D_MODEL), lambda l, f: (l, f, 0))
    return pl.pallas_call(
        _cast_kernel,
        grid=(DEPTH, D_FF // TF),
        in_specs=[blk],
        out_specs=blk,
        out_shape=jax.ShapeDtypeStruct(w_down.shape, BF16),
        compiler_params=_params("parallel", "parallel"),
        name="cast_down",
    )(w_down)


def _stationary_weights(w_ref, wb_ref):
    @pl.when(pl.program_id(1) == 0)
    def _():
        wb_ref[...] = w_ref[0].astype(BF16)


def _inproj_kv_kernel(a_ref, w_ref, *rest):
    z_ref, pc_ref, sc_ref, wb_ref = rest[-4:]
    _stationary_weights(w_ref, wb_ref)
    i = pl.program_id(1)
    r = jnp.dot(a_ref[...], wb_ref[...], preferred_element_type=F32)
    z_ref[...] = r.astype(BF16)

    @pl.when((i < P_TILES) & (i % TILES_PER_SEQ == TILES_PER_SEQ - 1))
    def _():
        pc_ref[0, 0] = r

    @pl.when(i == P_TILES)
    def _():
        sc_ref[0] = r


def _inproj_kv(h, w_in, layer, col0, p_cache, s_cache):
    blk0 = col0 // TN
    a_spec = pl.BlockSpec((TM, D_MODEL), lambda j, i: (i, 0))
    w_spec = pl.BlockSpec((1, D_MODEL, TN), lambda j, i: (layer, 0, j + blk0), pipeline_mode=pl.Buffered(1))
    in_specs, args, aliases = [a_spec, w_spec], [h, w_in], {}
    if p_cache is not None:
        in_specs += [_ANY, _ANY]
        args += [p_cache, s_cache]
        aliases = {2: 1, 3: 2}

    def p_map(j, i):
        return (layer, jnp.minimum(i, P_TILES - 1) // TILES_PER_SEQ, 0, j)

    return pl.pallas_call(
        _inproj_kv_kernel,
        grid=(WIDTH_ATT // TN, ROWS // TM),
        in_specs=in_specs,
        out_specs=[pl.BlockSpec((TM, TN), lambda j, i: (i, j)),
                   pl.BlockSpec((1, 1, N_KEEP, TN), p_map),
                   pl.BlockSpec((1, ROWS_S, TN), lambda j, i: (layer, 0, j))],
        out_shape=[jax.ShapeDtypeStruct((ROWS, WIDTH_ATT), BF16),
                   jax.ShapeDtypeStruct((DEPTH, BATCH, N_KEEP, WIDTH_ATT), F32),
                   jax.ShapeDtypeStruct((DEPTH, ROWS_S, WIDTH_ATT), F32)],
        scratch_shapes=[pltpu.VMEM((D_MODEL, TN), BF16)],
        input_output_aliases=aliases,
        compiler_params=_params("arbitrary", "arbitrary"),
        name="inproj_kv",
    )(*args)


def _inproj_rest_kernel(a_ref, w_ref, z_ref, wb_ref):
    _stationary_weights(w_ref, wb_ref)
    z_ref[...] = jnp.dot(a_ref[...], wb_ref[...], preferred_element_type=F32).astype(BF16)


def _inproj_rest(h, w_in, layer):
    n_q = WIDTH_ATT // TN_REST
    n_skip = 2 * WIDTH_ATT // TN_REST
    width = D_IN - 2 * WIDTH_ATT
    return pl.pallas_call(
        _inproj_rest_kernel,
        grid=(width // TN_REST, ROWS // TM_REST),
        in_specs=[pl.BlockSpec((TM_REST, D_MODEL), lambda j, i: (i, 0)),
                  pl.BlockSpec((1, D_MODEL, TN_REST),
                               lambda j, i: (layer, 0, jnp.where(j < n_q, j, j + n_skip)))],
        out_specs=pl.BlockSpec((TM_REST, TN_REST), lambda j, i: (i, j)),
        out_shape=jax.ShapeDtypeStruct((ROWS, width), BF16),
        scratch_shapes=[pltpu.VMEM((D_MODEL, TN_REST), BF16)],
        compiler_params=_params("arbitrary", "arbitrary"),
        name="inproj_rest",
    )(h, w_in)


def _softmax_pv(parts, vals):
    m = parts[0].max(axis=-1, keepdims=True)
    for s in parts[1:]:
        m = jnp.maximum(m, s.max(axis=-1, keepdims=True))
    acc = None
    l = None
    for s, v in zip(parts, vals):
        p = jnp.exp(s - m)
        ls = p.sum(axis=-1, keepdims=True)
        o = jnp.dot(p.astype(BF16), v, preferred_element_type=F32)
        acc = o if acc is None else acc + o
        l = ls if l is None else l + ls
    return acc / l


def _qk(q, k):
    return lax.dot_general(q, k, (((1,), (1,)), ((), ())), preferred_element_type=F32)


def _attn_prompt_kernel(q_ref, *refs):
    k_refs = refs[:ATT_KB]
    v_refs = refs[ATT_KB:2 * ATT_KB]
    bias_ref = refs[2 * ATT_KB]
    o_ref = refs[2 * ATT_KB + 1]
    scale = HEAD_DIM ** -0.5
    for h in range(N_HEADS):
        hs = slice(h * HEAD_DIM, (h + 1) * HEAD_DIM)
        k_all = jnp.concatenate([r[:, hs] for r in k_refs], axis=0)
        v_all = jnp.concatenate([r[:, hs] for r in v_refs], axis=0)
        s = _qk(q_ref[:, hs], k_all) * scale + bias_ref[0, h]
        o_ref[:, hs] = _softmax_pv([s], [v_all]).astype(o_ref.dtype)


def _attn_prompt(z_rest, z_k, z_v, bias):
    blocks_per_seq = SEQ // ATT_Q

    def kv_map(d):
        def index(n):
            first = (n // blocks_per_seq) * blocks_per_seq
            return (jnp.maximum(n - (ATT_KB - 1) + d, first), 0)
        return index

    blk = (ATT_Q, WIDTH_ATT)
    in_specs = [pl.BlockSpec(blk, lambda n: (n, 0))]
    in_specs += [pl.BlockSpec(blk, kv_map(d)) for d in range(ATT_KB)]
    in_specs += [pl.BlockSpec(blk, kv_map(d)) for d in range(ATT_KB)]
    in_specs += [pl.BlockSpec((1, N_HEADS, ATT_Q, ATT_KB * ATT_Q),
                              lambda n: (jnp.minimum(n % blocks_per_seq, ATT_KB - 1), 0, 0, 0))]
    return pl.pallas_call(
        _attn_prompt_kernel,
        grid=(ROWS_P // ATT_Q,),
        in_specs=in_specs,
        out_specs=pl.BlockSpec(blk, lambda n: (n, 0)),
        out_shape=jax.ShapeDtypeStruct((ROWS, WIDTH_ATT), BF16),
        compiler_params=_params("parallel"),
        name="attn_prompt",
    )(z_rest, *([z_k] * ATT_KB), *([z_v] * ATT_KB), bias)


def _attn_sample_kernel(q_ref, kn_ref, vn_ref, ck_ref, cv_ref, bias_c_ref, bias_n_ref, y_hbm, o_ref):
    del y_hbm
    scale = HEAD_DIM ** -0.5
    for h in range(N_HEADS):
        hs = slice(h * HEAD_DIM, (h + 1) * HEAD_DIM)
        q = q_ref[:, hs]
        s_c = _qk(q, ck_ref[0, 0, :, h, :].astype(BF16)) * scale + bias_c_ref[h]
        s_n = _qk(q, kn_ref[:, hs]) * scale + bias_n_ref[h]
        o = _softmax_pv([s_c, s_n], [cv_ref[0, 0, :, h, :].astype(BF16), vn_ref[:, hs]])
        o_ref[:, hs] = o.astype(o_ref.dtype)


def _attn_sample(z_rest, z_k, z_v, cache_k, cache_v, layer, bias_c, bias_n, y_att):
    blk0 = ROWS_P // DEC_SEQ
    n_cache = cache_k.shape[2]
    blk = (DEC_SEQ, WIDTH_ATT)
    rows = pl.BlockSpec(blk, lambda b: (b + blk0, 0))
    cache_spec = pl.BlockSpec((1, 1, n_cache, N_HEADS, HEAD_DIM), lambda b: (layer, b, 0, 0, 0))
    return pl.pallas_call(
        _attn_sample_kernel,
        grid=(DEC_BATCH,),
        in_specs=[rows, rows, rows, cache_spec, cache_spec,
                  pl.BlockSpec((N_HEADS, DEC_SEQ, n_cache), lambda b: (0, 0, 0)),
                  pl.BlockSpec((N_HEADS, DEC_SEQ, DEC_SEQ), lambda b: (0, 0, 0)),
                  _ANY],
        out_specs=rows,
        out_shape=jax.ShapeDtypeStruct((ROWS, WIDTH_ATT), BF16),
        input_output_aliases={7: 0},
        compiler_params=_params("parallel"),
        name="attn_sample",
    )(z_rest, z_k, z_v, cache_k, cache_v, bias_c, bias_n, y_att)


def _mixb_body(sb_ref, sc_ref, sh_ref, w_ref, o_ref, hist_out_ref, scr, rows):
    scr[HB_PAD:, :] = sc_ref[...].astype(F32) * sh_ref[...].astype(F32)
    for r0 in range(0, rows, CONV_CHUNK):
        acc = None
        for t in range(SC_WIDTH):
            start = r0 + HB_PAD - (SC_WIDTH - 1) + t
            term = scr[start:start + CONV_CHUNK, :] * w_ref[0, t:t + 1, :]
            acc = term if acc is None else acc + term
        y = sb_ref[r0:r0 + CONV_CHUNK, :].astype(F32) * acc
        o_ref[r0:r0 + CONV_CHUNK, :] = y.astype(o_ref.dtype)
    hist_out_ref[0] = scr[rows:rows + HB_PAD, :]


def _mixb_prompt_kernel(sb_ref, sc_ref, sh_ref, scp_ref, shp_ref, w_ref, o_ref, hist_out_ref, scr):
    first = pl.program_id(0) % (SEQ // MIX_ROWS) == 0
    prev = scp_ref[...].astype(F32) * shp_ref[...].astype(F32)
    prev = jnp.where(first, 0.0, prev)
    scr[0:HB_PAD, :] = prev[prev.shape[0] - HB_PAD:, :]
    _mixb_body(sb_ref, sc_ref, sh_ref, w_ref, o_ref, hist_out_ref, scr, MIX_ROWS)


def _mixb_sample_kernel(sb_ref, sc_ref, sh_ref, hist_ref, w_ref, y_hbm, o_ref, hist_out_ref, scr):
    del y_hbm
    scr[0:HB_PAD, :] = hist_ref[0]
    _mixb_body(sb_ref, sc_ref, sh_ref, w_ref, o_ref, hist_out_ref, scr, DEC_SEQ)


def _layer_rows(n, width, layer):
    return pl.BlockSpec((1, n, width), lambda i: (layer, 0, 0))


def _mixb_prompt(z_rest, w, layer):
    halo = 16
    tiles_per_seq = SEQ // MIX_ROWS
    cur = lambda c: pl.BlockSpec((MIX_ROWS, WIDTH_SC), lambda i: (i, c))
    prev = lambda c: pl.BlockSpec((halo, WIDTH_SC),
                                  lambda i: (jnp.maximum(i * (MIX_ROWS // halo) - 1, 0), c))
    return pl.pallas_call(
        _mixb_prompt_kernel,
        grid=(ROWS_P // MIX_ROWS,),
        in_specs=[cur(ZR_SB), cur(ZR_SC), cur(ZR_SH), prev(ZR_SC), prev(ZR_SH),
                  _layer_rows(SC_WIDTH, WIDTH_SC, layer)],
        out_specs=[pl.BlockSpec((MIX_ROWS, WIDTH_SC), lambda i: (i, 0)),
                   pl.BlockSpec((1, HB_PAD, WIDTH_SC), lambda i: (i // tiles_per_seq, 0, 0))],
        out_shape=[jax.ShapeDtypeStruct((ROWS, WIDTH_SC), BF16),
                   jax.ShapeDtypeStruct((BATCH, HB_PAD, WIDTH_SC), F32)],
        scratch_shapes=[pltpu.VMEM((HB_PAD + MIX_ROWS, WIDTH_SC), F32)],
        compiler_params=_params("arbitrary"),
        name="mixb_prompt",
    )(z_rest, z_rest, z_rest, z_rest, z_rest, w)


def _mixb_sample(z_rest, hist, w, layer, y_sc):
    blk0 = ROWS_P // DEC_SEQ
    cur = lambda c: pl.BlockSpec((DEC_SEQ, WIDTH_SC), lambda b: (b + blk0, c))
    return pl.pallas_call(
        _mixb_sample_kernel,
        grid=(DEC_BATCH,),
        in_specs=[cur(ZR_SB), cur(ZR_SC), cur(ZR_SH),
                  pl.BlockSpec((1, HB_PAD, WIDTH_SC), lambda b: (b, 0, 0)),
                  _layer_rows(SC_WIDTH, WIDTH_SC, layer),
                  _ANY],
        out_specs=[pl.BlockSpec((DEC_SEQ, WIDTH_SC), lambda b: (b + blk0, 0)),
                   pl.BlockSpec((1, HB_PAD, WIDTH_SC), lambda b: (b, 0, 0))],
        out_shape=[jax.ShapeDtypeStruct((ROWS, WIDTH_SC), BF16),
                   jax.ShapeDtypeStruct((DEC_BATCH, HB_PAD, WIDTH_SC), F32)],
        scratch_shapes=[pltpu.VMEM((HB_PAD + DEC_SEQ, WIDTH_SC), F32)],
        input_output_aliases={5: 0},
        compiler_params=_params("arbitrary"),
        name="mixb_sample",
    )(z_rest, z_rest, z_rest, hist, w, y_sc)


def _glu(a, g):
    return a.astype(F32) * jax.nn.sigmoid(g.astype(F32))


def _mixc_body(ca_ref, cg_ref, w_ref, cb_ref, lg_ref, lb_ref, o_ref, hist_out_ref, scr, rows):
    scr[HC_PAD:, :] = _glu(ca_ref[...], cg_ref[...])
    win_rows = CONV_CHUNK + HC_PAD
    lead = HC_PAD - (CV_WIDTH - 1)
    for r0 in range(0, rows, CONV_CHUNK):
        cols = []
        for c0 in range(0, WIDTH_CV, LANES):
            cs = slice(c0, c0 + LANES)
            win = scr[r0:r0 + win_rows, cs]
            acc = None
            for phase in range(SUBLANES):
                shifted = win if phase == 0 else pltpu.roll(win, win_rows - phase, axis=0)
                for base in range(0, win_rows - CONV_CHUNK + 1, SUBLANES):
                    t = base + phase - lead
                    if 0 <= t < CV_WIDTH:
                        term = shifted[base:base + CONV_CHUNK, :] * w_ref[0, t:t + 1, cs]
                        acc = term if acc is None else acc + term
            cols.append(acc)
        zc = jnp.concatenate(cols, axis=1) + cb_ref[0]
        mu = jnp.mean(zc, axis=-1, keepdims=True)
        cen = zc - mu
        var = jnp.mean(cen * cen, axis=-1, keepdims=True)
        y = cen * lax.rsqrt(var + EPS) * lg_ref[0] + lb_ref[0]
        o_ref[r0:r0 + CONV_CHUNK, :] = (y * jax.nn.sigmoid(y)).astype(o_ref.dtype)
    hist_out_ref[0] = scr[rows:rows + HC_PAD, :]


def _mixc_prompt_kernel(ca_ref, cg_ref, cap_ref, cgp_ref, w_ref, cb_ref, lg_ref, lb_ref,
                        o_ref, hist_out_ref, scr):
    first = pl.program_id(0) % (SEQ // MIX_ROWS) == 0
    scr[0:HC_PAD, :] = jnp.where(first, 0.0, _glu(cap_ref[...], cgp_ref[...]))
    _mixc_body(ca_ref, cg_ref, w_ref, cb_ref, lg_ref, lb_ref, o_ref, hist_out_ref, scr, MIX_ROWS)


def _mixc_sample_kernel(ca_ref, cg_ref, hist_ref, w_ref, cb_ref, lg_ref, lb_ref, y_hbm,
                        o_ref, hist_out_ref, scr):
    del y_hbm
    scr[0:HC_PAD, :] = hist_ref[0]
    _mixc_body(ca_ref, cg_ref, w_ref, cb_ref, lg_ref, lb_ref, o_ref, hist_out_ref, scr, DEC_SEQ)


def _mixc_prompt(z_rest, w, cb, lg, lb, layer):
    tiles_per_seq = SEQ // MIX_ROWS
    cur = lambda c: pl.BlockSpec((MIX_ROWS, WIDTH_CV), lambda i: (i, c))
    prev = lambda c: pl.BlockSpec((HC_PAD, WIDTH_CV),
                                  lambda i: (jnp.maximum(i * (MIX_ROWS // HC_PAD) - 1, 0), c))
    vec = _layer_rows(1, WIDTH_CV, layer)
    return pl.pallas_call(
        _mixc_prompt_kernel,
        grid=(ROWS_P // MIX_ROWS,),
        in_specs=[cur(ZR_CA), cur(ZR_CG), prev(ZR_CA), prev(ZR_CG),
                  _layer_rows(CV_WIDTH, WIDTH_CV, layer), vec, vec, vec],
        out_specs=[pl.BlockSpec((MIX_ROWS, WIDTH_CV), lambda i: (i, 0)),
                   pl.BlockSpec((1, HC_PAD, WIDTH_CV), lambda i: (i // tiles_per_seq, 0, 0))],
        out_shape=[jax.ShapeDtypeStruct((ROWS, WIDTH_CV), BF16),
                   jax.ShapeDtypeStruct((BATCH, HC_PAD, WIDTH_CV), F32)],
        scratch_shapes=[pltpu.VMEM((HC_PAD + MIX_ROWS, WIDTH_CV), F32)],
        compiler_params=_params("arbitrary"),
        name="mixc_prompt",
    )(z_rest, z_rest, z_rest, z_rest, w, cb, lg, lb)


def _mixc_sample(z_rest, hist, w, cb, lg, lb, layer, y_cv):
    blk0 = ROWS_P // DEC_SEQ
    cur = lambda c: pl.BlockSpec((DEC_SEQ, WIDTH_CV), lambda b: (b + blk0, c))
    vec = _layer_rows(1, WIDTH_CV, layer)
    return pl.pallas_call(
        _mixc_sample_kernel,
        grid=(DEC_BATCH,),
        in_specs=[cur(ZR_CA), cur(ZR_CG),
                  pl.BlockSpec((1, HC_PAD, WIDTH_CV), lambda b: (b, 0, 0)),
                  _layer_rows(CV_WIDTH, WIDTH_CV, layer), vec, vec, vec,
                  _ANY],
        out_specs=[pl.BlockSpec((DEC_SEQ, WIDTH_CV), lambda b: (b + blk0, 0)),
                   pl.BlockSpec((1, HC_PAD, WIDTH_CV), lambda b: (b, 0, 0))],
        out_shape=[jax.ShapeDtypeStruct((ROWS, WIDTH_CV), BF16),
                   jax.ShapeDtypeStruct((DEC_BATCH, HC_PAD, WIDTH_CV), F32)],
        scratch_shapes=[pltpu.VMEM((HC_PAD + DEC_SEQ, WIDTH_CV), F32)],
        input_output_aliases={7: 0},
        compiler_params=_params("arbitrary"),
        name="mixc_sample",
    )(z_rest, z_rest, hist, w, cb, lg, lb, y_cv)


def _outproj_kernel(ya_ref, yb_ref, yc_ref, w_ref, xp_ref, xs_ref, o_ref, wb_ref):
    _stationary_weights(w_ref, wb_ref)
    a0, a1 = WIDTH_ATT, WIDTH_ATT + WIDTH_SC
    acc = jnp.dot(ya_ref[...], wb_ref[:a0, :], preferred_element_type=F32)
    acc += jnp.dot(yb_ref[...], wb_ref[a0:a1, :], preferred_element_type=F32)
    acc += jnp.dot(yc_ref[...], wb_ref[a1:, :], preferred_element_type=F32)
    x = jnp.where(pl.program_id(1) < P_TILES, xp_ref[...], xs_ref[...])
    o_ref[...] = x + acc


def _outproj(y_att, y_sc, y_cv, w_out, layer, x_p, x_s, s_blk):
    row = lambda width: pl.BlockSpec((TM, width), lambda j, i: (i, 0))
    return pl.pallas_call(
        _outproj_kernel,
        grid=(D_MODEL // TN, ROWS // TM),
        in_specs=[row(WIDTH_ATT), row(WIDTH_SC), row(WIDTH_CV),
                  pl.BlockSpec((1, D_MODEL, TN), lambda j, i: (layer, 0, j), pipeline_mode=pl.Buffered(1)),
                  pl.BlockSpec((TM, TN), lambda j, i: (jnp.minimum(i, P_TILES - 1), j)),
                  pl.BlockSpec((TM, TN), lambda j, i: (s_blk, j))],
        out_specs=pl.BlockSpec((TM, TN), lambda j, i: (i, j)),
        out_shape=jax.ShapeDtypeStruct((ROWS, D_MODEL), F32),
        scratch_shapes=[pltpu.VMEM((D_MODEL, TN), BF16)],
        compiler_params=_params("arbitrary", "arbitrary"),
        name="outproj",
    )(y_att, y_sc, y_cv, w_out, x_p, x_s)


def _rmsnorm_rows(src_ref, g_ref, dst_ref, copy_ref=None):
    col_blocks = [slice(c0, c0 + NORM_COLS) for c0 in range(0, D_MODEL, NORM_COLS)]

    def chunk(c, carry):
        rows = pl.ds(pl.multiple_of(c * NORM_CHUNK, NORM_CHUNK), NORM_CHUNK)
        ss = None
        for cs in col_blocks:
            x = src_ref[rows, cs]
            ss = x * x if ss is None else ss + x * x
        inv = lax.rsqrt(jnp.sum(ss, axis=-1, keepdims=True) * (1.0 / D_MODEL) + EPS)
        for cs in col_blocks:
            x = src_ref[rows, cs]
            dst_ref[rows, cs] = (x * inv * g_ref[0, :, cs]).astype(dst_ref.dtype)
            if copy_ref is not None:
                copy_ref[rows, cs] = x
        return carry

    lax.fori_loop(0, TM // NORM_CHUNK, chunk, 0, unroll=4)


def _ffn_kernel(x_ref, g_ref, wgu_ref, wd_ref, *rest):
    h_scr = rest[-1]
    emit_next = len(rest) == 4
    o_ref = rest[1] if emit_next else rest[0]
    f = pl.program_id(1)

    @pl.when(f == 0)
    def _():
        _rmsnorm_rows(x_ref, g_ref, h_scr, copy_ref=o_ref)

    gu = jnp.dot(h_scr[...], wgu_ref[0, 0], preferred_element_type=F32)
    g = gu[:, :TF]
    u = gu[:, TF:]
    act = (g * jax.nn.sigmoid(g) * u).astype(BF16)
    o_ref[...] += jnp.dot(act, wd_ref[0], preferred_element_type=F32)

    if emit_next:
        @pl.when(f == D_FF // TF - 1)
        def _():
            _rmsnorm_rows(o_ref, rest[0], rest[2])


def _ffn(x, g, w_gu, w_down_b, layer, g_next=None):
    rows = pl.BlockSpec((TM, D_MODEL), lambda i, f: (i, 0))
    in_specs = [pl.BlockSpec((TM, D_MODEL), lambda i, f: (i, 0), pipeline_mode=pl.Buffered(1)),
                pl.BlockSpec((1, 1, D_MODEL), lambda i, f: (layer, 0, 0)),
                pl.BlockSpec((1, 1, D_MODEL, 2 * TF), lambda i, f: (layer, f, 0, 0)),
                pl.BlockSpec((1, TF, D_MODEL), lambda i, f: (layer, f, 0))]
    args = [x, g, w_gu, w_down_b]
    out_specs = [rows]
    out_shape = [jax.ShapeDtypeStruct((ROWS, D_MODEL), F32)]
    if g_next is not None:
        in_specs.append(pl.BlockSpec((1, 1, D_MODEL), lambda i, f: (layer + 1, 0, 0)))
        args.append(g_next)
        out_specs.append(rows)
        out_shape.append(jax.ShapeDtypeStruct((ROWS, D_MODEL), BF16))
    return pl.pallas_call(
        _ffn_kernel,
        grid=(ROWS // TM, D_FF // TF),
        in_specs=in_specs,
        out_specs=out_specs,
        out_shape=out_shape,
        scratch_shapes=[pltpu.VMEM((TM, D_MODEL), BF16)],
        compiler_params=_params("parallel", "arbitrary"),
        name="ffn",
    )(*args)


def _hankel(u, rows, cols):
    n_heads = u.shape[0]
    period = rows + cols
    u = jnp.pad(u, ((0, 0), (0, period - 1 - u.shape[1])))
    flat = jnp.broadcast_to(u[:, None, :], (n_heads, rows, period - 1)).reshape(n_heads, rows * (period - 1))
    flat = jnp.pad(flat, ((0, 0), (0, rows)))
    return flat.reshape(n_heads, rows, period)[:, :, :cols]


def _rel_bias_tile(rel_bias_l, rows, cols, n_past):
    t = jnp.arange(rows + cols - 1)
    dist = t + n_past - (cols - 1)
    u = rel_bias_l[:, jnp.clip(dist, -MAX_REL, MAX_REL) + MAX_REL]
    return _hankel(u, rows, cols)[:, :, ::-1].astype(F32)


def _prompt_bias(rel_bias_l):
    bias = _rel_bias_tile(rel_bias_l, ATT_Q, ATT_KB * ATT_Q, ATT_PAST)
    r = jnp.arange(ATT_Q)[:, None]
    c = jnp.arange(ATT_KB * ATT_Q)[None, :]
    q_chunk = r // CHUNK + LEFT_CHUNKS
    k_chunk = c // CHUNK
    band = (k_chunk <= q_chunk) & (k_chunk >= q_chunk - LEFT_CHUNKS)
    variant = jnp.arange(ATT_KB)[:, None, None]
    valid = band[None] & (c[None] // ATT_Q >= ATT_KB - 1 - variant)
    return jnp.where(valid[:, None], bias[None], NEG)


def _pad_hist(hist, pad_to):
    return jnp.pad(hist, ((0, 0), (pad_to - hist.shape[1], 0), (0, 0)))


def kernel(x_prompt, x_sample, cache_attn_k, cache_attn_v, cache_conv_b, cache_conv_c, norm_mix_g, w_in,
           rel_bias, conv_b_w, conv_c_w, conv_c_b, ln_c_g, ln_c_b, w_out, norm_ffn_g, w_ffn_gate, w_ffn_up,
           w_ffn_down, final_norm_g):
    n_cache = cache_attn_k.shape[2]
    x_p = x_prompt.reshape(ROWS_P, D_MODEL)
    x_s = x_sample.reshape(ROWS_S, D_MODEL)
    g_mix = norm_mix_g.reshape(DEPTH, 1, D_MODEL)
    g_ffn = norm_ffn_g.reshape(DEPTH, 1, D_MODEL)
    cb, lg, lb = (a.reshape(DEPTH, 1, WIDTH_CV) for a in (conv_c_b, ln_c_g, ln_c_b))
    w_gu = _cast_gate_up(w_ffn_gate, w_ffn_up)
    w_down_b = _cast_down(w_ffn_down)

    hists = {k: [] for k in ("pb", "pc", "sb", "sc")}
    pk = pv = sk = sv = None
    x = None
    h = _rmsnorm(x_p, norm_mix_g[0], BF16, 0, ROWS_P, ROWS)
    h = _rmsnorm(x_s, norm_mix_g[0], BF16, 0, ROWS_S, ROWS, dst_row0=ROWS_P, dst=h)
    for l in range(DEPTH):
        z_k, pk, sk = _inproj_kv(h, w_in, l, WIDTH_ATT, pk, sk)
        z_v, pv, sv = _inproj_kv(h, w_in, l, 2 * WIDTH_ATT, pv, sv)
        z_rest = _inproj_rest(h, w_in, l)

        y_att = _attn_prompt(z_rest, z_k, z_v, _prompt_bias(rel_bias[l]))
        bias_s = _rel_bias_tile(rel_bias[l], DEC_SEQ, n_cache + DEC_SEQ, n_cache)
        y_att = _attn_sample(z_rest, z_k, z_v, cache_attn_k, cache_attn_v, l,
                             bias_s[:, :, :n_cache], bias_s[:, :, n_cache:], y_att)

        y_sc, hb_p = _mixb_prompt(z_rest, conv_b_w, l)
        y_sc, hb_s = _mixb_sample(z_rest, _pad_hist(cache_conv_b[l], HB_PAD), conv_b_w, l, y_sc)
        y_cv, hc_p = _mixc_prompt(z_rest, conv_c_w, cb, lg, lb, l)
        y_cv, hc_s = _mixc_sample(z_rest, _pad_hist(cache_conv_c[l], HC_PAD), conv_c_w, cb, lg, lb, l, y_cv)

        if l == 0:
            x = _outproj(y_att, y_sc, y_cv, w_out, l, x_p, x_s, 0)
        else:
            x = _outproj(y_att, y_sc, y_cv, w_out, l, x, x, P_TILES)
        if l + 1 < DEPTH:
            x, h = _ffn(x, g_ffn, w_gu, w_down_b, l, g_next=g_mix)
        else:
            x, = _ffn(x, g_ffn, w_gu, w_down_b, l)

        hists["pb"].append(hb_p[:, HB_PAD - (SC_WIDTH - 1):])
        hists["pc"].append(hc_p[:, HC_PAD - (CV_WIDTH - 1):])
        hists["sb"].append(hb_s[:, HB_PAD - (SC_WIDTH - 1):])
        hists["sc"].append(hc_s[:, HC_PAD - (CV_WIDTH - 1):])

    y_prompt = _rmsnorm(x, final_norm_g, F32, 0, ROWS_P, ROWS_P).reshape(BATCH, SEQ, D_MODEL)
    y_sample = _rmsnorm(x, final_norm_g, F32, ROWS_P, ROWS_S, ROWS_S).reshape(DEC_BATCH, DEC_SEQ, D_MODEL)
    st = {k: jnp.stack(v) for k, v in hists.items()}
    heads_p = (DEPTH, BATCH, N_KEEP, N_HEADS, HEAD_DIM)
    heads_s = (DEPTH, DEC_BATCH, DEC_SEQ, N_HEADS, HEAD_DIM)
    return (y_prompt, y_sample, pk.reshape(heads_p), pv.reshape(heads_p), st["pb"], st["pc"],
            sk.reshape(heads_s), sv.reshape(heads_s), st["sb"], st["sc"])
```

```python
import jax
import jax.numpy as jnp
from jax import lax
from jax.experimental import pallas as pl
from jax.experimental.pallas import tpu as pltpu

F32 = jnp.float32
BF16 = jnp.bfloat16

D_MODEL = 4096
BATCH = 8
SEQ = 2048
DEPTH = 2
DEC_BATCH = 16
DEC_SEQ = 32
CHUNK = 64
LEFT_CHUNKS = 8
ATT_PAST = LEFT_CHUNKS * CHUNK
WIDTH_ATT = 2048
WIDTH_SC = 1024
WIDTH_CV = 1024
HEAD_DIM = 128
N_HEADS = 16
MAX_REL = 256
SC_WIDTH = 3
CV_WIDTH = 31
D_FF = 11008
D_IN = 11264
EPS = 1e-6

ROWS_P = BATCH * SEQ
ROWS_S = DEC_BATCH * DEC_SEQ
ROWS = ROWS_P + ROWS_S
N_KEEP = min(ATT_PAST, SEQ)

TM = 512
TN = 1024
TM_REST = ROWS // 16
TN_REST = 512
TF = 256
CAST_GU_ROWS = 32
CAST_D_ROWS = 64
NORM_ROWS = 256
NORM_CHUNK = 16
NORM_COLS = 512
ATT_Q = 256
ATT_KB = ATT_PAST // ATT_Q + 1
MIX_ROWS = 256
CONV_CHUNK = 32
HB_PAD = 8
HC_PAD = 32
NEG = -1e30
LANES = 128
SUBLANES = 8
VMEM_LIMIT = 56 * 1024 * 1024

P_TILES = ROWS_P // TM
TILES_PER_SEQ = SEQ // TM

ZR_SB, ZR_SC, ZR_SH, ZR_CA, ZR_CG = 2, 3, 4, 5, 6

assert TM == N_KEEP == ROWS_S and ROWS == (P_TILES + 1) * TM


def _params(*sem):
    return pltpu.CompilerParams(dimension_semantics=sem, vmem_limit_bytes=VMEM_LIMIT)


_ANY = pl.BlockSpec(memory_space=pl.ANY)


def _rmsnorm_kernel(x_ref, g_ref, *rest):
    o_ref = rest[-1]
    x = x_ref[...]
    ms = jnp.mean(x * x, axis=-1, keepdims=True)
    o_ref[...] = (x * lax.rsqrt(ms + EPS) * g_ref[...]).astype(o_ref.dtype)


def _rmsnorm(x, g, out_dtype, src_row0, rows, out_rows, dst_row0=0, dst=None):
    src0, dst0 = src_row0 // NORM_ROWS, dst_row0 // NORM_ROWS
    args = [x, g.reshape(1, D_MODEL)]
    in_specs = [pl.BlockSpec((NORM_ROWS, D_MODEL), lambda i: (i + src0, 0)),
                pl.BlockSpec((1, D_MODEL), lambda i: (0, 0))]
    aliases = {}
    if dst is not None:
        args.append(dst)
        in_specs.append(_ANY)
        aliases = {2: 0}
    return pl.pallas_call(
        _rmsnorm_kernel,
        grid=(rows // NORM_ROWS,),
        in_specs=in_specs,
        out_specs=pl.BlockSpec((NORM_ROWS, D_MODEL), lambda i: (i + dst0, 0)),
        out_shape=jax.ShapeDtypeStruct((out_rows, D_MODEL), out_dtype),
        input_output_aliases=aliases,
        compiler_params=_params("parallel"),
        name="rmsnorm",
    )(*args)


def _stationary_weights(w_ref, wb_ref):
    @pl.when(pl.program_id(1) == 0)
    def _():
        wb_ref[...] = w_ref[0].astype(BF16)


def _inproj_kv_kernel(a_ref, w_ref, *rest):
    z_ref, pc_ref, sc_ref, wb_ref = rest[-4:]
    _stationary_weights(w_ref, wb_ref)
    i = pl.program_id(1)
    r = jnp.dot(a_ref[...], wb_ref[...], preferred_element_type=F32)
    z_ref[...] = r.astype(BF16)

    @pl.when((i < P_TILES) & (i % TILES_PER_SEQ == TILES_PER_SEQ - 1))
    def _():
        pc_ref[0, 0] = r

    @pl.when(i == P_TILES)
    def _():
        sc_ref[0] = r


def _inproj_kv(h, w_in, layer, col0, p_cache, s_cache):
    blk0 = col0 // TN
    a_spec = pl.BlockSpec((TM, D_MODEL), lambda j, i: (i, 0))
    w_spec = pl.BlockSpec((1, D_MODEL, TN), lambda j, i: (layer, 0, j + blk0), pipeline_mode=pl.Buffered(1))
    in_specs, args, aliases = [a_spec, w_spec], [h, w_in], {}
    if p_cache is not None:
        in_specs += [_ANY, _ANY]
        args += [p_cache, s_cache]
        aliases = {2: 1, 3: 2}

    def p_map(j, i):
        return (layer, jnp.minimum(i, P_TILES - 1) // TILES_PER_SEQ, 0, j)

    return pl.pallas_call(
        _inproj_kv_kernel,
        grid=(WIDTH_ATT // TN, ROWS // TM),
        in_specs=in_specs,
        out_specs=[pl.BlockSpec((TM, TN), lambda j, i: (i, j)),
                   pl.BlockSpec((1, 1, N_KEEP, TN), p_map),
                   pl.BlockSpec((1, ROWS_S, TN), lambda j, i: (layer, 0, j))],
        out_shape=[jax.ShapeDtypeStruct((ROWS, WIDTH_ATT), BF16),
                   jax.ShapeDtypeStruct((DEPTH, BATCH, N_KEEP, WIDTH_ATT), F32),
                   jax.ShapeDtypeStruct((DEPTH, ROWS_S, WIDTH_ATT), F32)],
        scratch_shapes=[pltpu.VMEM((D_MODEL, TN), BF16)],
        input_output_aliases=aliases,
        compiler_params=_params("arbitrary", "arbitrary"),
        name="inproj_kv",
    )(*args)


def _inproj_rest_kernel(a_ref, w_ref, wg_ref, wu_ref, wd_ref, z_ref, wgu_out, wd_out, wb_ref):
    _stationary_weights(w_ref, wb_ref)
    z_ref[...] = jnp.dot(a_ref[...], wb_ref[...], preferred_element_type=F32).astype(BF16)
    for f in range(D_FF // TF):
        cs = slice(f * TF, (f + 1) * TF)
        wgu_out[f, :, :TF] = wg_ref[0, :, cs].astype(BF16)
        wgu_out[f, :, TF:] = wu_ref[0, :, cs].astype(BF16)
    wd_out[...] = wd_ref[0].astype(BF16)


def _inproj_rest(h, w_in, w_gate, w_up, w_down, layer):
    n_q = WIDTH_ATT // TN_REST
    n_skip = 2 * WIDTH_ATT // TN_REST
    width = D_IN - 2 * WIDTH_ATT
    n_i = ROWS // TM_REST
    n_gu = D_MODEL // CAST_GU_ROWS
    n_d = D_FF // CAST_D_ROWS
    assert max(n_gu, n_d) <= (width // TN_REST) * n_i
    gu_chunk = lambda j, i: jnp.minimum(j * n_i + i, n_gu - 1)
    d_chunk = lambda j, i: jnp.minimum(j * n_i + i, n_d - 1)
    gu_in = pl.BlockSpec((1, CAST_GU_ROWS, D_FF), lambda j, i: (layer, gu_chunk(j, i), 0))
    return pl.pallas_call(
        _inproj_rest_kernel,
        grid=(width // TN_REST, n_i),
        in_specs=[pl.BlockSpec((TM_REST, D_MODEL), lambda j, i: (i, 0)),
                  pl.BlockSpec((1, D_MODEL, TN_REST),
                               lambda j, i: (layer, 0, jnp.where(j < n_q, j, j + n_skip))),
                  gu_in, gu_in,
                  pl.BlockSpec((1, CAST_D_ROWS, D_MODEL), lambda j, i: (layer, d_chunk(j, i), 0))],
        out_specs=[pl.BlockSpec((TM_REST, TN_REST), lambda j, i: (i, j)),
                   pl.BlockSpec((D_FF // TF, CAST_GU_ROWS, 2 * TF), lambda j, i: (0, gu_chunk(j, i), 0)),
                   pl.BlockSpec((CAST_D_ROWS, D_MODEL), lambda j, i: (d_chunk(j, i), 0))],
        out_shape=[jax.ShapeDtypeStruct((ROWS, width), BF16),
                   jax.ShapeDtypeStruct((D_FF // TF, D_MODEL, 2 * TF), BF16),
                   jax.ShapeDtypeStruct((D_FF, D_MODEL), BF16)],
        scratch_shapes=[pltpu.VMEM((D_MODEL, TN_REST), BF16)],
        compiler_params=_params("arbitrary", "arbitrary"),
        name="inproj_rest",
    )(h, w_in, w_gate, w_up, w_down)


def _softmax_pv(parts, vals):
    m = parts[0].max(axis=-1, keepdims=True)
    for s in parts[1:]:
        m = jnp.maximum(m, s.max(axis=-1, keepdims=True))
    acc = None
    l = None
    for s, v in zip(parts, vals):
        p = jnp.exp(s - m)
        ls = p.sum(axis=-1, keepdims=True)
        o = jnp.dot(p.astype(BF16), v, preferred_element_type=F32)
        acc = o if acc is None else acc + o
        l = ls if l is None else l + ls
    return acc / l


def _qk(q, k):
    return lax.dot_general(q, k, (((1,), (1,)), ((), ())), preferred_element_type=F32)


def _attn_prompt_kernel(q_ref, *refs):
    k_refs = refs[:ATT_KB]
    v_refs = refs[ATT_KB:2 * ATT_KB]
    bias_ref = refs[2 * ATT_KB]
    o_ref = refs[2 * ATT_KB + 1]
    scale = HEAD_DIM ** -0.5
    for h in range(N_HEADS):
        hs = slice(h * HEAD_DIM, (h + 1) * HEAD_DIM)
        k_all = jnp.concatenate([r[:, hs] for r in k_refs], axis=0)
        v_all = jnp.concatenate([r[:, hs] for r in v_refs], axis=0)
        s = _qk(q_ref[:, hs], k_all) * scale + bias_ref[0, h]
        o_ref[:, hs] = _softmax_pv([s], [v_all]).astype(o_ref.dtype)


def _attn_prompt(z_rest, z_k, z_v, bias):
    blocks_per_seq = SEQ // ATT_Q

    def kv_map(d):
        def index(n):
            first = (n // blocks_per_seq) * blocks_per_seq
            return (jnp.maximum(n - (ATT_KB - 1) + d, first), 0)
        return index

    blk = (ATT_Q, WIDTH_ATT)
    in_specs = [pl.BlockSpec(blk, lambda n: (n, 0))]
    in_specs += [pl.BlockSpec(blk, kv_map(d)) for d in range(ATT_KB)]
    in_specs += [pl.BlockSpec(blk, kv_map(d)) for d in range(ATT_KB)]
    in_specs += [pl.BlockSpec((1, N_HEADS, ATT_Q, ATT_KB * ATT_Q),
                              lambda n: (jnp.minimum(n % blocks_per_seq, ATT_KB - 1), 0, 0, 0))]
    return pl.pallas_call(
        _attn_prompt_kernel,
        grid=(ROWS_P // ATT_Q,),
        in_specs=in_specs,
        out_specs=pl.BlockSpec(blk, lambda n: (n, 0)),
        out_shape=jax.ShapeDtypeStruct((ROWS, WIDTH_ATT), BF16),
        compiler_params=_params("parallel"),
        name="attn_prompt",
    )(z_rest, *([z_k] * ATT_KB), *([z_v] * ATT_KB), bias)


def _attn_sample_kernel(q_ref, kn_ref, vn_ref, ck_ref, cv_ref, bias_c_ref, bias_n_ref, y_hbm, o_ref):
    del y_hbm
    scale = HEAD_DIM ** -0.5
    for h in range(N_HEADS):
        hs = slice(h * HEAD_DIM, (h + 1) * HEAD_DIM)
        q = q_ref[:, hs]
        s_c = _qk(q, ck_ref[0, 0, :, h, :].astype(BF16)) * scale + bias_c_ref[h]
        s_n = _qk(q, kn_ref[:, hs]) * scale + bias_n_ref[h]
        o = _softmax_pv([s_c, s_n], [cv_ref[0, 0, :, h, :].astype(BF16), vn_ref[:, hs]])
        o_ref[:, hs] = o.astype(o_ref.dtype)


def _attn_sample(z_rest, z_k, z_v, cache_k, cache_v, layer, bias_c, bias_n, y_att):
    blk0 = ROWS_P // DEC_SEQ
    n_cache = cache_k.shape[2]
    blk = (DEC_SEQ, WIDTH_ATT)
    rows = pl.BlockSpec(blk, lambda b: (b + blk0, 0))
    cache_spec = pl.BlockSpec((1, 1, n_cache, N_HEADS, HEAD_DIM), lambda b: (layer, b, 0, 0, 0))
    return pl.pallas_call(
        _attn_sample_kernel,
        grid=(DEC_BATCH,),
        in_specs=[rows, rows, rows, cache_spec, cache_spec,
                  pl.BlockSpec((N_HEADS, DEC_SEQ, n_cache), lambda b: (0, 0, 0)),
                  pl.BlockSpec((N_HEADS, DEC_SEQ, DEC_SEQ), lambda b: (0, 0, 0)),
                  _ANY],
        out_specs=rows,
        out_shape=jax.ShapeDtypeStruct((ROWS, WIDTH_ATT), BF16),
        input_output_aliases={7: 0},
        compiler_params=_params("parallel"),
        name="attn_sample",
    )(z_rest, z_k, z_v, cache_k, cache_v, bias_c, bias_n, y_att)


def _mixb_body(sb_ref, sc_ref, sh_ref, w_ref, o_ref, hist_out_ref, scr, rows):
    scr[HB_PAD:, :] = sc_ref[...].astype(F32) * sh_ref[...].astype(F32)
    for r0 in range(0, rows, CONV_CHUNK):
        acc = None
        for t in range(SC_WIDTH):
            start = r0 + HB_PAD - (SC_WIDTH - 1) + t
            term = scr[start:start + CONV_CHUNK, :] * w_ref[0, t:t + 1, :]
            acc = term if acc is None else acc + term
        y = sb_ref[r0:r0 + CONV_CHUNK, :].astype(F32) * acc
        o_ref[r0:r0 + CONV_CHUNK, :] = y.astype(o_ref.dtype)
    hist_out_ref[0] = scr[rows:rows + HB_PAD, :]


def _mixb_prompt_kernel(sb_ref, sc_ref, sh_ref, scp_ref, shp_ref, w_ref, o_ref, hist_out_ref, scr):
    first = pl.program_id(0) % (SEQ // MIX_ROWS) == 0
    prev = scp_ref[...].astype(F32) * shp_ref[...].astype(F32)
    prev = jnp.where(first, 0.0, prev)
    scr[0:HB_PAD, :] = prev[prev.shape[0] - HB_PAD:, :]
    _mixb_body(sb_ref, sc_ref, sh_ref, w_ref, o_ref, hist_out_ref, scr, MIX_ROWS)


def _mixb_sample_kernel(sb_ref, sc_ref, sh_ref, hist_ref, w_ref, y_hbm, o_ref, hist_out_ref, scr):
    del y_hbm
    scr[0:HB_PAD, :] = hist_ref[0]
    _mixb_body(sb_ref, sc_ref, sh_ref, w_ref, o_ref, hist_out_ref, scr, DEC_SEQ)


def _layer_rows(n, width, layer):
    return pl.BlockSpec((1, n, width), lambda i: (layer, 0, 0))


def _mixb_prompt(z_rest, w, layer):
    halo = 16
    tiles_per_seq = SEQ // MIX_ROWS
    cur = lambda c: pl.BlockSpec((MIX_ROWS, WIDTH_SC), lambda i: (i, c))
    prev = lambda c: pl.BlockSpec((halo, WIDTH_SC),
                                  lambda i: (jnp.maximum(i * (MIX_ROWS // halo) - 1, 0), c))
    return pl.pallas_call(
        _mixb_prompt_kernel,
        grid=(ROWS_P // MIX_ROWS,),
        in_specs=[cur(ZR_SB), cur(ZR_SC), cur(ZR_SH), prev(ZR_SC), prev(ZR_SH),
                  _layer_rows(SC_WIDTH, WIDTH_SC, layer)],
        out_specs=[pl.BlockSpec((MIX_ROWS, WIDTH_SC), lambda i: (i, 0)),
                   pl.BlockSpec((1, HB_PAD, WIDTH_SC), lambda i: (i // tiles_per_seq, 0, 0))],
        out_shape=[jax.ShapeDtypeStruct((ROWS, WIDTH_SC), BF16),
                   jax.ShapeDtypeStruct((BATCH, HB_PAD, WIDTH_SC), F32)],
        scratch_shapes=[pltpu.VMEM((HB_PAD + MIX_ROWS, WIDTH_SC), F32)],
        compiler_params=_params("arbitrary"),
        name="mixb_prompt",
    )(z_rest, z_rest, z_rest, z_rest, z_rest, w)


def _mixb_sample(z_rest, hist, w, layer, y_sc):
    blk0 = ROWS_P // DEC_SEQ
    cur = lambda c: pl.BlockSpec((DEC_SEQ, WIDTH_SC), lambda b: (b + blk0, c))
    return pl.pallas_call(
        _mixb_sample_kernel,
        grid=(DEC_BATCH,),
        in_specs=[cur(ZR_SB), cur(ZR_SC), cur(ZR_SH),
                  pl.BlockSpec((1, HB_PAD, WIDTH_SC), lambda b: (b, 0, 0)),
                  _layer_rows(SC_WIDTH, WIDTH_SC, layer),
                  _ANY],
        out_specs=[pl.BlockSpec((DEC_SEQ, WIDTH_SC), lambda b: (b + blk0, 0)),
                   pl.BlockSpec((1, HB_PAD, WIDTH_SC), lambda b: (b, 0, 0))],
        out_shape=[jax.ShapeDtypeStruct((ROWS, WIDTH_SC), BF16),
                   jax.ShapeDtypeStruct((DEC_BATCH, HB_PAD, WIDTH_SC), F32)],
        scratch_shapes=[pltpu.VMEM((HB_PAD + DEC_SEQ, WIDTH_SC), F32)],
        input_output_aliases={5: 0},
        compiler_params=_params("arbitrary"),
        name="mixb_sample",
    )(z_rest, z_rest, z_rest, hist, w, y_sc)


def _glu(a, g):
    return a.astype(F32) * jax.nn.sigmoid(g.astype(F32))


def _mixc_body(ca_ref, cg_ref, w_ref, cb_ref, lg_ref, lb_ref, o_ref, hist_out_ref, scr, rows):
    scr[HC_PAD:, :] = _glu(ca_ref[...], cg_ref[...])
    win_rows = CONV_CHUNK + HC_PAD
    lead = HC_PAD - (CV_WIDTH - 1)
    for r0 in range(0, rows, CONV_CHUNK):
        cols = []
        for c0 in range(0, WIDTH_CV, LANES):
            cs = slice(c0, c0 + LANES)
            win = scr[r0:r0 + win_rows, cs]
            acc = None
            for phase in range(SUBLANES):
                shifted = win if phase == 0 else pltpu.roll(win, win_rows - phase, axis=0)
                for base in range(0, win_rows - CONV_CHUNK + 1, SUBLANES):
                    t = base + phase - lead
                    if 0 <= t < CV_WIDTH:
                        term = shifted[base:base + CONV_CHUNK, :] * w_ref[0, t:t + 1, cs]
                        acc = term if acc is None else acc + term
            cols.append(acc)
        zc = jnp.concatenate(cols, axis=1) + cb_ref[0]
        mu = jnp.mean(zc, axis=-1, keepdims=True)
        cen = zc - mu
        var = jnp.mean(cen * cen, axis=-1, keepdims=True)
        y = cen * lax.rsqrt(var + EPS) * lg_ref[0] + lb_ref[0]
        o_ref[r0:r0 + CONV_CHUNK, :] = (y * jax.nn.sigmoid(y)).astype(o_ref.dtype)
    hist_out_ref[0] = scr[rows:rows + HC_PAD, :]


def _mixc_prompt_kernel(ca_ref, cg_ref, cap_ref, cgp_ref, w_ref, cb_ref, lg_ref, lb_ref,
                        o_ref, hist_out_ref, scr):
    first = pl.program_id(0) % (SEQ // MIX_ROWS) == 0
    scr[0:HC_PAD, :] = jnp.where(first, 0.0, _glu(cap_ref[...], cgp_ref[...]))
    _mixc_body(ca_ref, cg_ref, w_ref, cb_ref, lg_ref, lb_ref, o_ref, hist_out_ref, scr, MIX_ROWS)


def _mixc_sample_kernel(ca_ref, cg_ref, hist_ref, w_ref, cb_ref, lg_ref, lb_ref, y_hbm,
                        o_ref, hist_out_ref, scr):
    del y_hbm
    scr[0:HC_PAD, :] = hist_ref[0]
    _mixc_body(ca_ref, cg_ref, w_ref, cb_ref, lg_ref, lb_ref, o_ref, hist_out_ref, scr, DEC_SEQ)


def _mixc_prompt(z_rest, w, cb, lg, lb, layer):
    tiles_per_seq = SEQ // MIX_ROWS
    cur = lambda c: pl.BlockSpec((MIX_ROWS, WIDTH_CV), lambda i: (i, c))
    prev = lambda c: pl.BlockSpec((HC_PAD, WIDTH_CV),
                                  lambda i: (jnp.maximum(i * (MIX_ROWS // HC_PAD) - 1, 0), c))
    vec = _layer_rows(1, WIDTH_CV, layer)
    return pl.pallas_call(
        _mixc_prompt_kernel,
        grid=(ROWS_P // MIX_ROWS,),
        in_specs=[cur(ZR_CA), cur(ZR_CG), prev(ZR_CA), prev(ZR_CG),
                  _layer_rows(CV_WIDTH, WIDTH_CV, layer), vec, vec, vec],
        out_specs=[pl.BlockSpec((MIX_ROWS, WIDTH_CV), lambda i: (i, 0)),
                   pl.BlockSpec((1, HC_PAD, WIDTH_CV), lambda i: (i // tiles_per_seq, 0, 0))],
        out_shape=[jax.ShapeDtypeStruct((ROWS, WIDTH_CV), BF16),
                   jax.ShapeDtypeStruct((BATCH, HC_PAD, WIDTH_CV), F32)],
        scratch_shapes=[pltpu.VMEM((HC_PAD + MIX_ROWS, WIDTH_CV), F32)],
        compiler_params=_params("arbitrary"),
        name="mixc_prompt",
    )(z_rest, z_rest, z_rest, z_rest, w, cb, lg, lb)


def _mixc_sample(z_rest, hist, w, cb, lg, lb, layer, y_cv):
    blk0 = ROWS_P // DEC_SEQ
    cur = lambda c: pl.BlockSpec((DEC_SEQ, WIDTH_CV), lambda b: (b + blk0, c))
    vec = _layer_rows(1, WIDTH_CV, layer)
    return pl.pallas_call(
        _mixc_sample_kernel,
        grid=(DEC_BATCH,),
        in_specs=[cur(ZR_CA), cur(ZR_CG),
                  pl.BlockSpec((1, HC_PAD, WIDTH_CV), lambda b: (b, 0, 0)),
                  _layer_rows(CV_WIDTH, WIDTH_CV, layer), vec, vec, vec,
                  _ANY],
        out_specs=[pl.BlockSpec((DEC_SEQ, WIDTH_CV), lambda b: (b + blk0, 0)),
                   pl.BlockSpec((1, HC_PAD, WIDTH_CV), lambda b: (b, 0, 0))],
        out_shape=[jax.ShapeDtypeStruct((ROWS, WIDTH_CV), BF16),
                   jax.ShapeDtypeStruct((DEC_BATCH, HC_PAD, WIDTH_CV), F32)],
        scratch_shapes=[pltpu.VMEM((HC_PAD + DEC_SEQ, WIDTH_CV), F32)],
        input_output_aliases={7: 0},
        compiler_params=_params("arbitrary"),
        name="mixc_sample",
    )(z_rest, z_rest, hist, w, cb, lg, lb, y_cv)


def _outproj_kernel(ya_ref, yb_ref, yc_ref, w_ref, xp_ref, xs_ref, o_ref, wb_ref):
    _stationary_weights(w_ref, wb_ref)
    a0, a1 = WIDTH_ATT, WIDTH_ATT + WIDTH_SC
    acc = jnp.dot(ya_ref[...], wb_ref[:a0, :], preferred_element_type=F32)
    acc += jnp.dot(yb_ref[...], wb_ref[a0:a1, :], preferred_element_type=F32)
    acc += jnp.dot(yc_ref[...], wb_ref[a1:, :], preferred_element_type=F32)
    x = jnp.where(pl.program_id(1) < P_TILES, xp_ref[...], xs_ref[...])
    o_ref[...] = x + acc


def _outproj(y_att, y_sc, y_cv, w_out, layer, x_p, x_s, s_blk):
    row = lambda width: pl.BlockSpec((TM, width), lambda j, i: (i, 0))
    return pl.pallas_call(
        _outproj_kernel,
        grid=(D_MODEL // TN, ROWS // TM),
        in_specs=[row(WIDTH_ATT), row(WIDTH_SC), row(WIDTH_CV),
                  pl.BlockSpec((1, D_MODEL, TN), lambda j, i: (layer, 0, j), pipeline_mode=pl.Buffered(1)),
                  pl.BlockSpec((TM, TN), lambda j, i: (jnp.minimum(i, P_TILES - 1), j)),
                  pl.BlockSpec((TM, TN), lambda j, i: (s_blk, j))],
        out_specs=pl.BlockSpec((TM, TN), lambda j, i: (i, j)),
        out_shape=jax.ShapeDtypeStruct((ROWS, D_MODEL), F32),
        scratch_shapes=[pltpu.VMEM((D_MODEL, TN), BF16)],
        compiler_params=_params("arbitrary", "arbitrary"),
        name="outproj",
    )(y_att, y_sc, y_cv, w_out, x_p, x_s)


def _rmsnorm_rows(src_ref, g_ref, dst_ref, copy_ref=None):
    col_blocks = [slice(c0, c0 + NORM_COLS) for c0 in range(0, D_MODEL, NORM_COLS)]

    def chunk(c, carry):
        rows = pl.ds(pl.multiple_of(c * NORM_CHUNK, NORM_CHUNK), NORM_CHUNK)
        ss = None
        for cs in col_blocks:
            x = src_ref[rows, cs]
            ss = x * x if ss is None else ss + x * x
        inv = lax.rsqrt(jnp.sum(ss, axis=-1, keepdims=True) * (1.0 / D_MODEL) + EPS)
        for cs in col_blocks:
            x = src_ref[rows, cs]
            dst_ref[rows, cs] = (x * inv * g_ref[0, :, cs]).astype(dst_ref.dtype)
            if copy_ref is not None:
                copy_ref[rows, cs] = x
        return carry

    lax.fori_loop(0, TM // NORM_CHUNK, chunk, 0, unroll=4)


def _ffn_kernel(x_ref, g_ref, wgu_ref, wd_ref, *rest):
    h_scr = rest[-1]
    emit_next = len(rest) == 4
    o_ref = rest[1] if emit_next else rest[0]
    f = pl.program_id(1)

    @pl.when(f == 0)
    def _():
        _rmsnorm_rows(x_ref, g_ref, h_scr, copy_ref=o_ref)

    gu = jnp.dot(h_scr[...], wgu_ref[0], preferred_element_type=F32)
    g = gu[:, :TF]
    u = gu[:, TF:]
    act = (g * jax.nn.sigmoid(g) * u).astype(BF16)
    o_ref[...] += jnp.dot(act, wd_ref[...], preferred_element_type=F32)

    if emit_next:
        @pl.when(f == D_FF // TF - 1)
        def _():
            _rmsnorm_rows(o_ref, rest[0], rest[2])


def _ffn(x, g, w_gu, w_down_b, layer, g_next=None):
    rows = pl.BlockSpec((TM, D_MODEL), lambda i, f: (i, 0))
    in_specs = [pl.BlockSpec((TM, D_MODEL), lambda i, f: (i, 0), pipeline_mode=pl.Buffered(1)),
                pl.BlockSpec((1, 1, D_MODEL), lambda i, f: (layer, 0, 0)),
                pl.BlockSpec((1, D_MODEL, 2 * TF), lambda i, f: (f, 0, 0)),
                pl.BlockSpec((TF, D_MODEL), lambda i, f: (f, 0))]
    args = [x, g, w_gu, w_down_b]
    out_specs = [rows]
    out_shape = [jax.ShapeDtypeStruct((ROWS, D_MODEL), F32)]
    if g_next is not None:
        in_specs.append(pl.BlockSpec((1, 1, D_MODEL), lambda i, f: (layer + 1, 0, 0)))
        args.append(g_next)
        out_specs.append(rows)
        out_shape.append(jax.ShapeDtypeStruct((ROWS, D_MODEL), BF16))
    return pl.pallas_call(
        _ffn_kernel,
        grid=(ROWS // TM, D_FF // TF),
        in_specs=in_specs,
        out_specs=out_specs,
        out_shape=out_shape,
        scratch_shapes=[pltpu.VMEM((TM, D_MODEL), BF16)],
        compiler_params=_params("parallel", "arbitrary"),
        name="ffn",
    )(*args)


def _rel_bias_tile(rel_bias_l, rows, cols, n_past):
    n_heads = rel_bias_l.shape[0]
    period = rows + cols
    k = jnp.arange(period)
    dist = n_past - jnp.where(k < cols, k, k - period)
    u = rel_bias_l[:, jnp.clip(dist, -MAX_REL, MAX_REL) + MAX_REL].astype(F32)
    flat = jnp.broadcast_to(u[:, None, :], (n_heads, rows, period)).reshape(n_heads, rows * period)
    return flat[:, :rows * (period - 1)].reshape(n_heads, rows, period - 1)[:, :, :cols]


def _prompt_bias(rel_bias_l):
    bias = _rel_bias_tile(rel_bias_l, ATT_Q, ATT_KB * ATT_Q, ATT_PAST)
    r = jnp.arange(ATT_Q)[:, None]
    c = jnp.arange(ATT_KB * ATT_Q)[None, :]
    q_chunk = r // CHUNK + LEFT_CHUNKS
    k_chunk = c // CHUNK
    band = (k_chunk <= q_chunk) & (k_chunk >= q_chunk - LEFT_CHUNKS)
    variant = jnp.arange(ATT_KB)[:, None, None]
    valid = band[None] & (c[None] // ATT_Q >= ATT_KB - 1 - variant)
    return jnp.where(valid[:, None], bias[None], NEG)


def _pad_hist(hist, pad_to):
    return jnp.pad(hist, ((0, 0), (pad_to - hist.shape[1], 0), (0, 0)))


def kernel(x_prompt, x_sample, cache_attn_k, cache_attn_v, cache_conv_b, cache_conv_c, norm_mix_g, w_in,
           rel_bias, conv_b_w, conv_c_w, conv_c_b, ln_c_g, ln_c_b, w_out, norm_ffn_g, w_ffn_gate, w_ffn_up,
           w_ffn_down, final_norm_g):
    n_cache = cache_attn_k.shape[2]
    x_p = x_prompt.reshape(ROWS_P, D_MODEL)
    x_s = x_sample.reshape(ROWS_S, D_MODEL)
    g_mix = norm_mix_g.reshape(DEPTH, 1, D_MODEL)
    g_ffn = norm_ffn_g.reshape(DEPTH, 1, D_MODEL)
    cb, lg, lb = (a.reshape(DEPTH, 1, WIDTH_CV) for a in (conv_c_b, ln_c_g, ln_c_b))

    hists = {k: [] for k in ("pb", "pc", "sb", "sc")}
    pk = pv = sk = sv = None
    x = None
    h = _rmsnorm(x_p, norm_mix_g[0], BF16, 0, ROWS_P, ROWS)
    h = _rmsnorm(x_s, norm_mix_g[0], BF16, 0, ROWS_S, ROWS, dst_row0=ROWS_P, dst=h)
    for l in range(DEPTH):
        z_k, pk, sk = _inproj_kv(h, w_in, l, WIDTH_ATT, pk, sk)
        z_v, pv, sv = _inproj_kv(h, w_in, l, 2 * WIDTH_ATT, pv, sv)
        z_rest, w_gu, w_down_b = _inproj_rest(h, w_in, w_ffn_gate, w_ffn_up, w_ffn_down, l)

        y_att = _attn_prompt(z_rest, z_k, z_v, _prompt_bias(rel_bias[l]))
        bias_s = _rel_bias_tile(rel_bias[l], DEC_SEQ, n_cache + DEC_SEQ, n_cache)
        y_att = _attn_sample(z_rest, z_k, z_v, cache_attn_k, cache_attn_v, l,
                             bias_s[:, :, :n_cache], bias_s[:, :, n_cache:], y_att)

        y_sc, hb_p = _mixb_prompt(z_rest, conv_b_w, l)
        y_sc, hb_s = _mixb_sample(z_rest, _pad_hist(cache_conv_b[l], HB_PAD), conv_b_w, l, y_sc)
        y_cv, hc_p = _mixc_prompt(z_rest, conv_c_w, cb, lg, lb, l)
        y_cv, hc_s = _mixc_sample(z_rest, _pad_hist(cache_conv_c[l], HC_PAD), conv_c_w, cb, lg, lb, l, y_cv)

        if l == 0:
            x = _outproj(y_att, y_sc, y_cv, w_out, l, x_p, x_s, 0)
        else:
            x = _outproj(y_att, y_sc, y_cv, w_out, l, x, x, P_TILES)
        if l + 1 < DEPTH:
            x, h = _ffn(x, g_ffn, w_gu, w_down_b, l, g_next=g_mix)
        else:
            x, = _ffn(x, g_ffn, w_gu, w_down_b, l)

        hists["pb"].append(hb_p[:, HB_PAD - (SC_WIDTH - 1):])
        hists["pc"].append(hc_p[:, HC_PAD - (CV_WIDTH - 1):])
        hists["sb"].append(hb_s[:, HB_PAD - (SC_WIDTH - 1):])
        hists["sc"].append(hc_s[:, HC_PAD - (CV_WIDTH - 1):])

    y_prompt = _rmsnorm(x, final_norm_g, F32, 0, ROWS_P, ROWS_P).reshape(BATCH, SEQ, D_MODEL)
    y_sample = _rmsnorm(x, final_norm_g, F32, ROWS_P, ROWS_S, ROWS_S).reshape(DEC_BATCH, DEC_SEQ, D_MODEL)
    st = {k: jnp.stack(v) for k, v in hists.items()}
    heads_p = (DEPTH, BATCH, N_KEEP, N_HEADS, HEAD_DIM)
    heads_s = (DEPTH, DEC_BATCH, DEC_SEQ, N_HEADS, HEAD_DIM)
    return (y_prompt, y_sample, pk.reshape(heads_p), pv.reshape(heads_p), st["pb"], st["pc"],
            sk.reshape(heads_s), sv.reshape(heads_s), st["sb"], st["sc"])
```

```python
import jax
import jax.numpy as jnp
from jax import lax
from jax.experimental import pallas as pl
from jax.experimental.pallas import tpu as pltpu

F32 = jnp.float32
BF16 = jnp.bfloat16

D_MODEL = 4096
BATCH = 8
SEQ = 2048
DEPTH = 2
DEC_BATCH = 16
DEC_SEQ = 32
CHUNK = 64
LEFT_CHUNKS = 8
ATT_PAST = LEFT_CHUNKS * CHUNK
WIDTH_ATT = 2048
WIDTH_SC = 1024
WIDTH_CV = 1024
HEAD_DIM = 128
N_HEADS = 16
MAX_REL = 256
SC_WIDTH = 3
CV_WIDTH = 31
D_FF = 11008
D_IN = 11264
EPS = 1e-6

ROWS_P = BATCH * SEQ
ROWS_S = DEC_BATCH * DEC_SEQ
ROWS = ROWS_P + ROWS_S
N_KEEP = min(ATT_PAST, SEQ)

TM = 512
TN = 1024
TM_REST = ROWS // 16
TN_REST = 512
TF = 512
N_F = -(-D_FF // TF)
D_FF_PAD = N_F * TF
CAST_GU_ROWS = 32
CAST_D_ROWS = 64
N_CAST_GU = D_MODEL // CAST_GU_ROWS
N_CAST_D = D_FF // CAST_D_ROWS
N_CAST_D_PAD = D_FF_PAD // CAST_D_ROWS
NORM_ROWS = 256
NORM_CHUNK = 16
NORM_COLS = 512
ATT_Q = 256
ATT_KB = ATT_PAST // ATT_Q + 1
MIX_ROWS = 256
CONV_CHUNK = 32
HB_PAD = 8
HC_PAD = 32
NEG = -1e30
LANES = 128
SUBLANES = 8
VMEM_LIMIT = 56 * 1024 * 1024
VMEM_LIMIT_FFN = 60 * 1024 * 1024

P_TILES = ROWS_P // TM
TILES_PER_SEQ = SEQ // TM

ZR_SB, ZR_SC, ZR_SH, ZR_CA, ZR_CG = 2, 3, 4, 5, 6

assert TM == N_KEEP == ROWS_S and ROWS == (P_TILES + 1) * TM


def _params(*sem, vmem=VMEM_LIMIT):
    return pltpu.CompilerParams(dimension_semantics=sem, vmem_limit_bytes=vmem)


_ANY = pl.BlockSpec(memory_space=pl.ANY)


def _rmsnorm_kernel(x_ref, g_ref, *rest):
    o_ref = rest[-1]
    x = x_ref[...]
    ms = jnp.mean(x * x, axis=-1, keepdims=True)
    o_ref[...] = (x * lax.rsqrt(ms + EPS) * g_ref[...]).astype(o_ref.dtype)


def _rmsnorm(x, g, out_dtype, src_row0, rows, out_rows, dst_row0=0, dst=None):
    src0, dst0 = src_row0 // NORM_ROWS, dst_row0 // NORM_ROWS
    args = [x, g.reshape(1, D_MODEL)]
    in_specs = [pl.BlockSpec((NORM_ROWS, D_MODEL), lambda i: (i + src0, 0)),
                pl.BlockSpec((1, D_MODEL), lambda i: (0, 0))]
    aliases = {}
    if dst is not None:
        args.append(dst)
        in_specs.append(_ANY)
        aliases = {2: 0}
    return pl.pallas_call(
        _rmsnorm_kernel,
        grid=(rows // NORM_ROWS,),
        in_specs=in_specs,
        out_specs=pl.BlockSpec((NORM_ROWS, D_MODEL), lambda i: (i + dst0, 0)),
        out_shape=jax.ShapeDtypeStruct((out_rows, D_MODEL), out_dtype),
        input_output_aliases=aliases,
        compiler_params=_params("parallel"),
        name="rmsnorm",
    )(*args)


def _stationary_weights(w_ref, wb_ref):
    @pl.when(pl.program_id(1) == 0)
    def _():
        wb_ref[...] = w_ref[0].astype(BF16)


def _inproj_kv_kernel(a_ref, w_ref, *rest):
    z_ref, pc_ref, sc_ref, wb_ref = rest[-4:]
    _stationary_weights(w_ref, wb_ref)
    i = pl.program_id(1)
    r = jnp.dot(a_ref[...], wb_ref[...], preferred_element_type=F32)
    z_ref[...] = r.astype(BF16)

    @pl.when((i < P_TILES) & (i % TILES_PER_SEQ == TILES_PER_SEQ - 1))
    def _():
        pc_ref[0, 0] = r

    @pl.when(i == P_TILES)
    def _():
        sc_ref[0] = r


def _inproj_kv(h, w_in, layer, col0, p_cache, s_cache):
    blk0 = col0 // TN
    a_spec = pl.BlockSpec((TM, D_MODEL), lambda j, i: (i, 0))
    w_spec = pl.BlockSpec((1, D_MODEL, TN), lambda j, i: (layer, 0, j + blk0), pipeline_mode=pl.Buffered(1))
    in_specs, args, aliases = [a_spec, w_spec], [h, w_in], {}
    if p_cache is not None:
        in_specs += [_ANY, _ANY]
        args += [p_cache, s_cache]
        aliases = {2: 1, 3: 2}

    def p_map(j, i):
        return (layer, jnp.minimum(i, P_TILES - 1) // TILES_PER_SEQ, 0, j)

    return pl.pallas_call(
        _inproj_kv_kernel,
        grid=(WIDTH_ATT // TN, ROWS // TM),
        in_specs=in_specs,
        out_specs=[pl.BlockSpec((TM, TN), lambda j, i: (i, j)),
                   pl.BlockSpec((1, 1, N_KEEP, TN), p_map),
                   pl.BlockSpec((1, ROWS_S, TN), lambda j, i: (layer, 0, j))],
        out_shape=[jax.ShapeDtypeStruct((ROWS, WIDTH_ATT), BF16),
                   jax.ShapeDtypeStruct((DEPTH, BATCH, N_KEEP, WIDTH_ATT), F32),
                   jax.ShapeDtypeStruct((DEPTH, ROWS_S, WIDTH_ATT), F32)],
        scratch_shapes=[pltpu.VMEM((D_MODEL, TN), BF16)],
        input_output_aliases=aliases,
        compiler_params=_params("arbitrary", "arbitrary"),
        name="inproj_kv",
    )(*args)


def _inproj_rest_kernel(a_ref, w_ref, wg_ref, wu_ref, wd_ref, z_ref, wgu_out, wd_out, wb_ref):
    _stationary_weights(w_ref, wb_ref)
    z_ref[...] = jnp.dot(a_ref[...], wb_ref[...], preferred_element_type=F32).astype(BF16)
    step = pl.program_id(0) * pl.num_programs(1) + pl.program_id(1)

    @pl.when(step < N_CAST_GU)
    def _():
        for f in range(N_F):
            c0 = f * TF
            valid = min(TF, D_FF - c0)
            for src_ref, lane0 in ((wg_ref, 0), (wu_ref, TF)):
                wgu_out[f, :, lane0:lane0 + valid] = src_ref[0, :, c0:c0 + valid].astype(BF16)
                if valid < TF:
                    wgu_out[f, :, lane0 + valid:lane0 + TF] = jnp.zeros((CAST_GU_ROWS, TF - valid), BF16)

    @pl.when(step < N_CAST_D)
    def _():
        wd_out[...] = wd_ref[0].astype(BF16)

    @pl.when((step >= N_CAST_D) & (step < N_CAST_D_PAD))
    def _():
        wd_out[...] = jnp.zeros(wd_out.shape, BF16)


def _inproj_rest(h, w_in, w_gate, w_up, w_down, layer):
    n_q = WIDTH_ATT // TN_REST
    n_skip = 2 * WIDTH_ATT // TN_REST
    width = D_IN - 2 * WIDTH_ATT
    n_i = ROWS // TM_REST
    assert max(N_CAST_GU, N_CAST_D_PAD) <= (width // TN_REST) * n_i
    gu_chunk = lambda j, i: jnp.minimum(j * n_i + i, N_CAST_GU - 1)
    d_chunk_in = lambda j, i: jnp.minimum(j * n_i + i, N_CAST_D - 1)
    d_chunk = lambda j, i: jnp.minimum(j * n_i + i, N_CAST_D_PAD - 1)
    gu_in = pl.BlockSpec((1, CAST_GU_ROWS, D_FF), lambda j, i: (layer, gu_chunk(j, i), 0))
    return pl.pallas_call(
        _inproj_rest_kernel,
        grid=(width // TN_REST, n_i),
        in_specs=[pl.BlockSpec((TM_REST, D_MODEL), lambda j, i: (i, 0)),
                  pl.BlockSpec((1, D_MODEL, TN_REST),
                               lambda j, i: (layer, 0, jnp.where(j < n_q, j, j + n_skip))),
                  gu_in, gu_in,
                  pl.BlockSpec((1, CAST_D_ROWS, D_MODEL), lambda j, i: (layer, d_chunk_in(j, i), 0))],
        out_specs=[pl.BlockSpec((TM_REST, TN_REST), lambda j, i: (i, j)),
                   pl.BlockSpec((N_F, CAST_GU_ROWS, 2 * TF), lambda j, i: (0, gu_chunk(j, i), 0)),
                   pl.BlockSpec((CAST_D_ROWS, D_MODEL), lambda j, i: (d_chunk(j, i), 0))],
        out_shape=[jax.ShapeDtypeStruct((ROWS, width), BF16),
                   jax.ShapeDtypeStruct((N_F, D_MODEL, 2 * TF), BF16),
                   jax.ShapeDtypeStruct((D_FF_PAD, D_MODEL), BF16)],
        scratch_shapes=[pltpu.VMEM((D_MODEL, TN_REST), BF16)],
        compiler_params=_params("arbitrary", "arbitrary"),
        name="inproj_rest",
    )(h, w_in, w_gate, w_up, w_down)


def _softmax_pv(parts, vals):
    m = parts[0].max(axis=-1, keepdims=True)
    for s in parts[1:]:
        m = jnp.maximum(m, s.max(axis=-1, keepdims=True))
    acc = None
    l = None
    for s, v in zip(parts, vals):
        p = jnp.exp(s - m)
        ls = p.sum(axis=-1, keepdims=True)
        o = jnp.dot(p.astype(BF16), v, preferred_element_type=F32)
        acc = o if acc is None else acc + o
        l = ls if l is None else l + ls
    return acc / l


def _qk(q, k):
    return lax.dot_general(q, k, (((1,), (1,)), ((), ())), preferred_element_type=F32)


def _attn_prompt_kernel(q_ref, *refs):
    k_refs = refs[:ATT_KB]
    v_refs = refs[ATT_KB:2 * ATT_KB]
    bias_ref = refs[2 * ATT_KB]
    o_ref = refs[2 * ATT_KB + 1]
    scale = HEAD_DIM ** -0.5
    for h in range(N_HEADS):
        hs = slice(h * HEAD_DIM, (h + 1) * HEAD_DIM)
        k_all = jnp.concatenate([r[:, hs] for r in k_refs], axis=0)
        v_all = jnp.concatenate([r[:, hs] for r in v_refs], axis=0)
        s = _qk(q_ref[:, hs], k_all) * scale + bias_ref[0, h]
        o_ref[:, hs] = _softmax_pv([s], [v_all]).astype(o_ref.dtype)


def _attn_prompt(z_rest, z_k, z_v, bias):
    blocks_per_seq = SEQ // ATT_Q

    def kv_map(d):
        def index(n):
            first = (n // blocks_per_seq) * blocks_per_seq
            return (jnp.maximum(n - (ATT_KB - 1) + d, first), 0)
        return index

    blk = (ATT_Q, WIDTH_ATT)
    in_specs = [pl.BlockSpec(blk, lambda n: (n, 0))]
    in_specs += [pl.BlockSpec(blk, kv_map(d)) for d in range(ATT_KB)]
    in_specs += [pl.BlockSpec(blk, kv_map(d)) for d in range(ATT_KB)]
    in_specs += [pl.BlockSpec((1, N_HEADS, ATT_Q, ATT_KB * ATT_Q),
                              lambda n: (jnp.minimum(n % blocks_per_seq, ATT_KB - 1), 0, 0, 0))]
    return pl.pallas_call(
        _attn_prompt_kernel,
        grid=(ROWS_P // ATT_Q,),
        in_specs=in_specs,
        out_specs=pl.BlockSpec(blk, lambda n: (n, 0)),
        out_shape=jax.ShapeDtypeStruct((ROWS, WIDTH_ATT), BF16),
        compiler_params=_params("parallel"),
        name="attn_prompt",
    )(z_rest, *([z_k] * ATT_KB), *([z_v] * ATT_KB), bias)


def _attn_sample_kernel(q_ref, kn_ref, vn_ref, ck_ref, cv_ref, bias_c_ref, bias_n_ref, y_hbm, o_ref):
    del y_hbm
    scale = HEAD_DIM ** -0.5
    for h in range(N_HEADS):
        hs = slice(h * HEAD_DIM, (h + 1) * HEAD_DIM)
        q = q_ref[:, hs]
        s_c = _qk(q, ck_ref[0, 0, :, h, :].astype(BF16)) * scale + bias_c_ref[h]
        s_n = _qk(q, kn_ref[:, hs]) * scale + bias_n_ref[h]
        o = _softmax_pv([s_c, s_n], [cv_ref[0, 0, :, h, :].astype(BF16), vn_ref[:, hs]])
        o_ref[:, hs] = o.astype(o_ref.dtype)


def _attn_sample(z_rest, z_k, z_v, cache_k, cache_v, layer, bias_c, bias_n, y_att):
    blk0 = ROWS_P // DEC_SEQ
    n_cache = cache_k.shape[2]
    blk = (DEC_SEQ, WIDTH_ATT)
    rows = pl.BlockSpec(blk, lambda b: (b + blk0, 0))
    cache_spec = pl.BlockSpec((1, 1, n_cache, N_HEADS, HEAD_DIM), lambda b: (layer, b, 0, 0, 0))
    return pl.pallas_call(
        _attn_sample_kernel,
        grid=(DEC_BATCH,),
        in_specs=[rows, rows, rows, cache_spec, cache_spec,
                  pl.BlockSpec((N_HEADS, DEC_SEQ, n_cache), lambda b: (0, 0, 0)),
                  pl.BlockSpec((N_HEADS, DEC_SEQ, DEC_SEQ), lambda b: (0, 0, 0)),
                  _ANY],
        out_specs=rows,
        out_shape=jax.ShapeDtypeStruct((ROWS, WIDTH_ATT), BF16),
        input_output_aliases={7: 0},
        compiler_params=_params("parallel"),
        name="attn_sample",
    )(z_rest, z_k, z_v, cache_k, cache_v, bias_c, bias_n, y_att)


def _mixb_body(sb_ref, sc_ref, sh_ref, w_ref, o_ref, hist_out_ref, scr, rows):
    scr[HB_PAD:, :] = sc_ref[...].astype(F32) * sh_ref[...].astype(F32)
    for r0 in range(0, rows, CONV_CHUNK):
        acc = None
        for t in range(SC_WIDTH):
            start = r0 + HB_PAD - (SC_WIDTH - 1) + t
            term = scr[start:start + CONV_CHUNK, :] * w_ref[0, t:t + 1, :]
            acc = term if acc is None else acc + term
        y = sb_ref[r0:r0 + CONV_CHUNK, :].astype(F32) * acc
        o_ref[r0:r0 + CONV_CHUNK, :] = y.astype(o_ref.dtype)
    hist_out_ref[0] = scr[rows:rows + HB_PAD, :]


def _mixb_prompt_kernel(sb_ref, sc_ref, sh_ref, scp_ref, shp_ref, w_ref, o_ref, hist_out_ref, scr):
    first = pl.program_id(0) % (SEQ // MIX_ROWS) == 0
    prev = scp_ref[...].astype(F32) * shp_ref[...].astype(F32)
    prev = jnp.where(first, 0.0, prev)
    scr[0:HB_PAD, :] = prev[prev.shape[0] - HB_PAD:, :]
    _mixb_body(sb_ref, sc_ref, sh_ref, w_ref, o_ref, hist_out_ref, scr, MIX_ROWS)


def _mixb_sample_kernel(sb_ref, sc_ref, sh_ref, hist_ref, w_ref, y_hbm, o_ref, hist_out_ref, scr):
    del y_hbm
    scr[0:HB_PAD, :] = hist_ref[0]
    _mixb_body(sb_ref, sc_ref, sh_ref, w_ref, o_ref, hist_out_ref, scr, DEC_SEQ)


def _layer_rows(n, width, layer):
    return pl.BlockSpec((1, n, width), lambda i: (layer, 0, 0))


def _mixb_prompt(z_rest, w, layer):
    halo = 16
    tiles_per_seq = SEQ // MIX_ROWS
    cur = lambda c: pl.BlockSpec((MIX_ROWS, WIDTH_SC), lambda i: (i, c))
    prev = lambda c: pl.BlockSpec((halo, WIDTH_SC),
                                  lambda i: (jnp.maximum(i * (MIX_ROWS // halo) - 1, 0), c))
    return pl.pallas_call(
        _mixb_prompt_kernel,
        grid=(ROWS_P // MIX_ROWS,),
        in_specs=[cur(ZR_SB), cur(ZR_SC), cur(ZR_SH), prev(ZR_SC), prev(ZR_SH),
                  _layer_rows(SC_WIDTH, WIDTH_SC, layer)],
        out_specs=[pl.BlockSpec((MIX_ROWS, WIDTH_SC), lambda i: (i, 0)),
                   pl.BlockSpec((1, HB_PAD, WIDTH_SC), lambda i: (i // tiles_per_seq, 0, 0))],
        out_shape=[jax.ShapeDtypeStruct((ROWS, WIDTH_SC), BF16),
                   jax.ShapeDtypeStruct((BATCH, HB_PAD, WIDTH_SC), F32)],
        scratch_shapes=[pltpu.VMEM((HB_PAD + MIX_ROWS, WIDTH_SC), F32)],
        compiler_params=_params("arbitrary"),
        name="mixb_prompt",
    )(z_rest, z_rest, z_rest, z_rest, z_rest, w)


def _mixb_sample(z_rest, hist, w, layer, y_sc):
    blk0 = ROWS_P // DEC_SEQ
    cur = lambda c: pl.BlockSpec((DEC_SEQ, WIDTH_SC), lambda b: (b + blk0, c))
    return pl.pallas_call(
        _mixb_sample_kernel,
        grid=(DEC_BATCH,),
        in_specs=[cur(ZR_SB), cur(ZR_SC), cur(ZR_SH),
                  pl.BlockSpec((1, HB_PAD, WIDTH_SC), lambda b: (b, 0, 0)),
                  _layer_rows(SC_WIDTH, WIDTH_SC, layer),
                  _ANY],
        out_specs=[pl.BlockSpec((DEC_SEQ, WIDTH_SC), lambda b: (b + blk0, 0)),
                   pl.BlockSpec((1, HB_PAD, WIDTH_SC), lambda b: (b, 0, 0))],
        out_shape=[jax.ShapeDtypeStruct((ROWS, WIDTH_SC), BF16),
                   jax.ShapeDtypeStruct((DEC_BATCH, HB_PAD, WIDTH_SC), F32)],
        scratch_shapes=[pltpu.VMEM((HB_PAD + DEC_SEQ, WIDTH_SC), F32)],
        input_output_aliases={5: 0},
        compiler_params=_params("arbitrary"),
        name="mixb_sample",
    )(z_rest, z_rest, z_rest, hist, w, y_sc)


def _glu(a, g):
    return a.astype(F32) * jax.nn.sigmoid(g.astype(F32))


def _mixc_body(ca_ref, cg_ref, w_ref, cb_ref, lg_ref, lb_ref, o_ref, hist_out_ref, scr, rows):
    scr[HC_PAD:, :] = _glu(ca_ref[...], cg_ref[...])
    win_rows = CONV_CHUNK + HC_PAD
    lead = HC_PAD - (CV_WIDTH - 1)
    for r0 in range(0, rows, CONV_CHUNK):
        cols = []
        for c0 in range(0, WIDTH_CV, LANES):
            cs = slice(c0, c0 + LANES)
            win = scr[r0:r0 + win_rows, cs]
            acc = None
            for phase in range(SUBLANES):
                shifted = win if phase == 0 else pltpu.roll(win, win_rows - phase, axis=0)
                for base in range(0, win_rows - CONV_CHUNK + 1, SUBLANES):
                    t = base + phase - lead
                    if 0 <= t < CV_WIDTH:
                        term = shifted[base:base + CONV_CHUNK, :] * w_ref[0, t:t + 1, cs]
                        acc = term if acc is None else acc + term
            cols.append(acc)
        zc = jnp.concatenate(cols, axis=1) + cb_ref[0]
        mu = jnp.mean(zc, axis=-1, keepdims=True)
        cen = zc - mu
        var = jnp.mean(cen * cen, axis=-1, keepdims=True)
        y = cen * lax.rsqrt(var + EPS) * lg_ref[0] + lb_ref[0]
        o_ref[r0:r0 + CONV_CHUNK, :] = (y * jax.nn.sigmoid(y)).astype(o_ref.dtype)
    hist_out_ref[0] = scr[rows:rows + HC_PAD, :]


def _mixc_prompt_kernel(ca_ref, cg_ref, cap_ref, cgp_ref, w_ref, cb_ref, lg_ref, lb_ref,
                        o_ref, hist_out_ref, scr):
    first = pl.program_id(0) % (SEQ // MIX_ROWS) == 0
    scr[0:HC_PAD, :] = jnp.where(first, 0.0, _glu(cap_ref[...], cgp_ref[...]))
    _mixc_body(ca_ref, cg_ref, w_ref, cb_ref, lg_ref, lb_ref, o_ref, hist_out_ref, scr, MIX_ROWS)


def _mixc_sample_kernel(ca_ref, cg_ref, hist_ref, w_ref, cb_ref, lg_ref, lb_ref, y_hbm,
                        o_ref, hist_out_ref, scr):
    del y_hbm
    scr[0:HC_PAD, :] = hist_ref[0]
    _mixc_body(ca_ref, cg_ref, w_ref, cb_ref, lg_ref, lb_ref, o_ref, hist_out_ref, scr, DEC_SEQ)


def _mixc_prompt(z_rest, w, cb, lg, lb, layer):
    tiles_per_seq = SEQ // MIX_ROWS
    cur = lambda c: pl.BlockSpec((MIX_ROWS, WIDTH_CV), lambda i: (i, c))
    prev = lambda c: pl.BlockSpec((HC_PAD, WIDTH_CV),
                                  lambda i: (jnp.maximum(i * (MIX_ROWS // HC_PAD) - 1, 0), c))
    vec = _layer_rows(1, WIDTH_CV, layer)
    return pl.pallas_call(
        _mixc_prompt_kernel,
        grid=(ROWS_P // MIX_ROWS,),
        in_specs=[cur(ZR_CA), cur(ZR_CG), prev(ZR_CA), prev(ZR_CG),
                  _layer_rows(CV_WIDTH, WIDTH_CV, layer), vec, vec, vec],
        out_specs=[pl.BlockSpec((MIX_ROWS, WIDTH_CV), lambda i: (i, 0)),
                   pl.BlockSpec((1, HC_PAD, WIDTH_CV), lambda i: (i // tiles_per_seq, 0, 0))],
        out_shape=[jax.ShapeDtypeStruct((ROWS, WIDTH_CV), BF16),
                   jax.ShapeDtypeStruct((BATCH, HC_PAD, WIDTH_CV), F32)],
        scratch_shapes=[pltpu.VMEM((HC_PAD + MIX_ROWS, WIDTH_CV), F32)],
        compiler_params=_params("arbitrary"),
        name="mixc_prompt",
    )(z_rest, z_rest, z_rest, z_rest, w, cb, lg, lb)


def _mixc_sample(z_rest, hist, w, cb, lg, lb, layer, y_cv):
    blk0 = ROWS_P // DEC_SEQ
    cur = lambda c: pl.BlockSpec((DEC_SEQ, WIDTH_CV), lambda b: (b + blk0, c))
    vec = _layer_rows(1, WIDTH_CV, layer)
    return pl.pallas_call(
        _mixc_sample_kernel,
        grid=(DEC_BATCH,),
        in_specs=[cur(ZR_CA), cur(ZR_CG),
                  pl.BlockSpec((1, HC_PAD, WIDTH_CV), lambda b: (b, 0, 0)),
                  _layer_rows(CV_WIDTH, WIDTH_CV, layer), vec, vec, vec,
                  _ANY],
        out_specs=[pl.BlockSpec((DEC_SEQ, WIDTH_CV), lambda b: (b + blk0, 0)),
                   pl.BlockSpec((1, HC_PAD, WIDTH_CV), lambda b: (b, 0, 0))],
        out_shape=[jax.ShapeDtypeStruct((ROWS, WIDTH_CV), BF16),
                   jax.ShapeDtypeStruct((DEC_BATCH, HC_PAD, WIDTH_CV), F32)],
        scratch_shapes=[pltpu.VMEM((HC_PAD + DEC_SEQ, WIDTH_CV), F32)],
        input_output_aliases={7: 0},
        compiler_params=_params("arbitrary"),
        name="mixc_sample",
    )(z_rest, z_rest, hist, w, cb, lg, lb, y_cv)


def _outproj_kernel(ya_ref, yb_ref, yc_ref, w_ref, xp_ref, xs_ref, o_ref, wb_ref):
    _stationary_weights(w_ref, wb_ref)
    a0, a1 = WIDTH_ATT, WIDTH_ATT + WIDTH_SC
    acc = jnp.dot(ya_ref[...], wb_ref[:a0, :], preferred_element_type=F32)
    acc += jnp.dot(yb_ref[...], wb_ref[a0:a1, :], preferred_element_type=F32)
    acc += jnp.dot(yc_ref[...], wb_ref[a1:, :], preferred_element_type=F32)
    x = jnp.where(pl.program_id(1) < P_TILES, xp_ref[...], xs_ref[...])
    o_ref[...] = x + acc


def _outproj(y_att, y_sc, y_cv, w_out, layer, x_p, x_s, s_blk):
    row = lambda width: pl.BlockSpec((TM, width), lambda j, i: (i, 0))
    return pl.pallas_call(
        _outproj_kernel,
        grid=(D_MODEL // TN, ROWS // TM),
        in_specs=[row(WIDTH_ATT), row(WIDTH_SC), row(WIDTH_CV),
                  pl.BlockSpec((1, D_MODEL, TN), lambda j, i: (layer, 0, j), pipeline_mode=pl.Buffered(1)),
                  pl.BlockSpec((TM, TN), lambda j, i: (jnp.minimum(i, P_TILES - 1), j)),
                  pl.BlockSpec((TM, TN), lambda j, i: (s_blk, j))],
        out_specs=pl.BlockSpec((TM, TN), lambda j, i: (i, j)),
        out_shape=jax.ShapeDtypeStruct((ROWS, D_MODEL), F32),
        scratch_shapes=[pltpu.VMEM((D_MODEL, TN), BF16)],
        compiler_params=_params("arbitrary", "arbitrary"),
        name="outproj",
    )(y_att, y_sc, y_cv, w_out, x_p, x_s)


def _rmsnorm_rows(src_ref, g_ref, dst_ref, copy_ref=None):
    col_blocks = [slice(c0, c0 + NORM_COLS) for c0 in range(0, D_MODEL, NORM_COLS)]

    def chunk(c, carry):
        rows = pl.ds(pl.multiple_of(c * NORM_CHUNK, NORM_CHUNK), NORM_CHUNK)
        ss = None
        for cs in col_blocks:
            x = src_ref[rows, cs]
            ss = x * x if ss is None else ss + x * x
        inv = lax.rsqrt(jnp.sum(ss, axis=-1, keepdims=True) * (1.0 / D_MODEL) + EPS)
        for cs in col_blocks:
            x = src_ref[rows, cs]
            dst_ref[rows, cs] = (x * inv * g_ref[0, :, cs]).astype(dst_ref.dtype)
            if copy_ref is not None:
                copy_ref[rows, cs] = x
        return carry

    lax.fori_loop(0, TM // NORM_CHUNK, chunk, 0, unroll=4)


def _ffn_kernel(x_ref, g_ref, wgu_ref, wd_ref, o_ref, h_scr):
    @pl.when(pl.program_id(1) == 0)
    def _():
        _rmsnorm_rows(x_ref, g_ref, h_scr, copy_ref=o_ref)

    gu = jnp.dot(h_scr[...], wgu_ref[0], preferred_element_type=F32)
    g = gu[:, :TF]
    u = gu[:, TF:]
    act = (g * jax.nn.sigmoid(g) * u).astype(BF16)
    o_ref[...] += jnp.dot(act, wd_ref[...], preferred_element_type=F32)


def _ffn(x, g, w_gu, w_down_b, layer):
    return pl.pallas_call(
        _ffn_kernel,
        grid=(ROWS // TM, N_F),
        in_specs=[pl.BlockSpec((TM, D_MODEL), lambda i, f: (i, 0), pipeline_mode=pl.Buffered(1)),
                  pl.BlockSpec((1, 1, D_MODEL), lambda i, f: (layer, 0, 0)),
                  pl.BlockSpec((1, D_MODEL, 2 * TF), lambda i, f: (f, 0, 0)),
                  pl.BlockSpec((TF, D_MODEL), lambda i, f: (f, 0))],
        out_specs=pl.BlockSpec((TM, D_MODEL), lambda i, f: (i, 0)),
        out_shape=jax.ShapeDtypeStruct((ROWS, D_MODEL), F32),
        scratch_shapes=[pltpu.VMEM((TM, D_MODEL), BF16)],
        compiler_params=_params("parallel", "arbitrary", vmem=VMEM_LIMIT_FFN),
        name="ffn",
    )(x, g, w_gu, w_down_b)


def _rel_bias_tile(rel_bias_l, rows, cols, n_past):
    n_heads = rel_bias_l.shape[0]
    period = rows + cols
    k = jnp.arange(period)
    dist = n_past - jnp.where(k < cols, k, k - period)
    u = rel_bias_l[:, jnp.clip(dist, -MAX_REL, MAX_REL) + MAX_REL].astype(F32)
    flat = jnp.broadcast_to(u[:, None, :], (n_heads, rows, period)).reshape(n_heads, rows * period)
    return flat[:, :rows * (period - 1)].reshape(n_heads, rows, period - 1)[:, :, :cols]


def _prompt_bias(rel_bias_l):
    bias = _rel_bias_tile(rel_bias_l, ATT_Q, ATT_KB * ATT_Q, ATT_PAST)
    r = jnp.arange(ATT_Q)[:, None]
    c = jnp.arange(ATT_KB * ATT_Q)[None, :]
    q_chunk = r // CHUNK + LEFT_CHUNKS
    k_chunk = c // CHUNK
    band = (k_chunk <= q_chunk) & (k_chunk >= q_chunk - LEFT_CHUNKS)
    variant = jnp.arange(ATT_KB)[:, None, None]
    valid = band[None] & (c[None] // ATT_Q >= ATT_KB - 1 - variant)
    return jnp.where(valid[:, None], bias[None], NEG)


def _pad_hist(hist, pad_to):
    return jnp.pad(hist, ((0, 0), (pad_to - hist.shape[1], 0), (0, 0)))


def kernel(x_prompt, x_sample, cache_attn_k, cache_attn_v, cache_conv_b, cache_conv_c, norm_mix_g, w_in,
           rel_bias, conv_b_w, conv_c_w, conv_c_b, ln_c_g, ln_c_b, w_out, norm_ffn_g, w_ffn_gate, w_ffn_up,
           w_ffn_down, final_norm_g):
    n_cache = cache_attn_k.shape[2]
    x_p = x_prompt.reshape(ROWS_P, D_MODEL)
    x_s = x_sample.reshape(ROWS_S, D_MODEL)
    g_ffn = norm_ffn_g.reshape(DEPTH, 1, D_MODEL)
    cb, lg, lb = (a.reshape(DEPTH, 1, WIDTH_CV) for a in (conv_c_b, ln_c_g, ln_c_b))

    hists = {k: [] for k in ("pb", "pc", "sb", "sc")}
    pk = pv = sk = sv = None
    x = None
    h = _rmsnorm(x_p, norm_mix_g[0], BF16, 0, ROWS_P, ROWS)
    h = _rmsnorm(x_s, norm_mix_g[0], BF16, 0, ROWS_S, ROWS, dst_row0=ROWS_P, dst=h)
    for l in range(DEPTH):
        z_k, pk, sk = _inproj_kv(h, w_in, l, WIDTH_ATT, pk, sk)
        z_v, pv, sv = _inproj_kv(h, w_in, l, 2 * WIDTH_ATT, pv, sv)
        z_rest, w_gu, w_down_b = _inproj_rest(h, w_in, w_ffn_gate, w_ffn_up, w_ffn_down, l)

        y_att = _attn_prompt(z_rest, z_k, z_v, _prompt_bias(rel_bias[l]))
        bias_s = _rel_bias_tile(rel_bias[l], DEC_SEQ, n_cache + DEC_SEQ, n_cache)
        y_att = _attn_sample(z_rest, z_k, z_v, cache_attn_k, cache_attn_v, l,
                             bias_s[:, :, :n_cache], bias_s[:, :, n_cache:], y_att)

        y_sc, hb_p = _mixb_prompt(z_rest, conv_b_w, l)
        y_sc, hb_s = _mixb_sample(z_rest, _pad_hist(cache_conv_b[l], HB_PAD), conv_b_w, l, y_sc)
        y_cv, hc_p = _mixc_prompt(z_rest, conv_c_w, cb, lg, lb, l)
        y_cv, hc_s = _mixc_sample(z_rest, _pad_hist(cache_conv_c[l], HC_PAD), conv_c_w, cb, lg, lb, l, y_cv)

        if l == 0:
            x = _outproj(y_att, y_sc, y_cv, w_out, l, x_p, x_s, 0)
        else:
            x = _outproj(y_att, y_sc, y_cv, w_out, l, x, x, P_TILES)
        x = _ffn(x, g_ffn, w_gu, w_down_b, l)
        if l + 1 < DEPTH:
            h = _rmsnorm(x, norm_mix_g[l + 1], BF16, 0, ROWS, ROWS)

        hists["pb"].append(hb_p[:, HB_PAD - (SC_WIDTH - 1):])
        hists["pc"].append(hc_p[:, HC_PAD - (CV_WIDTH - 1):])
        hists["sb"].append(hb_s[:, HB_PAD - (SC_WIDTH - 1):])
        hists["sc"].append(hc_s[:, HC_PAD - (CV_WIDTH - 1):])

    y_prompt = _rmsnorm(x, final_norm_g, F32, 0, ROWS_P, ROWS_P).reshape(BATCH, SEQ, D_MODEL)
    y_sample = _rmsnorm(x, final_norm_g, F32, ROWS_P, ROWS_S, ROWS_S).reshape(DEC_BATCH, DEC_SEQ, D_MODEL)
    st = {k: jnp.stack(v) for k, v in hists.items()}
    heads_p = (DEPTH, BATCH, N_KEEP, N_HEADS, HEAD_DIM)
    heads_s = (DEPTH, DEC_BATCH, DEC_SEQ, N_HEADS, HEAD_DIM)
    return (y_prompt, y_sample, pk.reshape(heads_p), pv.reshape(heads_p), st["pb"], st["pc"],
            sk.reshape(heads_s), sv.reshape(heads_s), st["sb"], st["sc"])
```

```python
import jax
import jax.numpy as jnp
from jax import lax
from jax.experimental import pallas as pl
from jax.experimental.pallas import tpu as pltpu

F32 = jnp.float32
BF16 = jnp.bfloat16

D_MODEL = 4096
BATCH = 8
SEQ = 2048
DEPTH = 2
DEC_BATCH = 16
DEC_SEQ = 32
CHUNK = 64
LEFT_CHUNKS = 8
ATT_PAST = LEFT_CHUNKS * CHUNK
WIDTH_ATT = 2048
WIDTH_SC = 1024
WIDTH_CV = 1024
HEAD_DIM = 128
N_HEADS = 16
MAX_REL = 256
SC_WIDTH = 3
CV_WIDTH = 31
D_FF = 11008
D_IN = 11264
EPS = 1e-6

ROWS_P = BATCH * SEQ
ROWS_S = DEC_BATCH * DEC_SEQ
ROWS = ROWS_P + ROWS_S
N_KEEP = min(ATT_PAST, SEQ)

TM = 512
TN = 1024
TM_REST = ROWS // 16
TN_REST = 512
TF = 512
N_F = -(-D_FF // TF)
D_FF_PAD = N_F * TF
CAST_GU_ROWS = 32
CAST_D_ROWS = 64
N_CAST_GU = D_MODEL // CAST_GU_ROWS
N_CAST_D = D_FF // CAST_D_ROWS
N_CAST_D_PAD = D_FF_PAD // CAST_D_ROWS
NORM_ROWS = 256
NORM_CHUNK = 16
NORM_COLS = 512
ATT_Q = 256
ATT_KB = ATT_PAST // ATT_Q + 1
MIX_ROWS = 256
CONV_CHUNK = 32
HB_PAD = 8
HC_PAD = 32
NEG = -1e30
LANES = 128
SUBLANES = 8
VMEM_LIMIT = 56 * 1024 * 1024
VMEM_LIMIT_FFN = 60 * 1024 * 1024

P_TILES = ROWS_P // TM
TILES_PER_SEQ = SEQ // TM

ZR_SB, ZR_SC, ZR_SH, ZR_CA, ZR_CG = 2, 3, 4, 5, 6

assert TM == N_KEEP == ROWS_S and ROWS == (P_TILES + 1) * TM


def _params(*sem, vmem=VMEM_LIMIT):
    return pltpu.CompilerParams(dimension_semantics=sem, vmem_limit_bytes=vmem)


_ANY = pl.BlockSpec(memory_space=pl.ANY)


def _rmsnorm_kernel(x_ref, g_ref, *rest):
    o_ref = rest[-1]
    x = x_ref[...]
    ms = jnp.mean(x * x, axis=-1, keepdims=True)
    o_ref[...] = (x * lax.rsqrt(ms + EPS) * g_ref[...]).astype(o_ref.dtype)


def _rmsnorm(x, g, out_dtype, src_row0, rows, out_rows, dst_row0=0, dst=None):
    src0, dst0 = src_row0 // NORM_ROWS, dst_row0 // NORM_ROWS
    args = [x, g.reshape(1, D_MODEL)]
    in_specs = [pl.BlockSpec((NORM_ROWS, D_MODEL), lambda i: (i + src0, 0)),
                pl.BlockSpec((1, D_MODEL), lambda i: (0, 0))]
    aliases = {}
    if dst is not None:
        args.append(dst)
        in_specs.append(_ANY)
        aliases = {2: 0}
    return pl.pallas_call(
        _rmsnorm_kernel,
        grid=(rows // NORM_ROWS,),
        in_specs=in_specs,
        out_specs=pl.BlockSpec((NORM_ROWS, D_MODEL), lambda i: (i + dst0, 0)),
        out_shape=jax.ShapeDtypeStruct((out_rows, D_MODEL), out_dtype),
        input_output_aliases=aliases,
        compiler_params=_params("parallel"),
        name="rmsnorm",
    )(*args)


def _stationary_weights(w_ref, wb_ref):
    @pl.when(pl.program_id(1) == 0)
    def _():
        wb_ref[...] = w_ref[0].astype(BF16)


def _inproj_kv_kernel(a_ref, w_ref, *rest):
    z_ref, pc_ref, sc_ref, wb_ref = rest[-4:]
    _stationary_weights(w_ref, wb_ref)
    i = pl.program_id(1)
    r = jnp.dot(a_ref[...], wb_ref[...], preferred_element_type=F32)
    z_ref[...] = r.astype(BF16)

    @pl.when((i < P_TILES) & (i % TILES_PER_SEQ == TILES_PER_SEQ - 1))
    def _():
        pc_ref[0, 0] = r

    @pl.when(i == P_TILES)
    def _():
        sc_ref[0] = r


def _inproj_kv(h, w_in, layer, col0, p_cache, s_cache):
    blk0 = col0 // TN
    a_spec = pl.BlockSpec((TM, D_MODEL), lambda j, i: (i, 0))
    w_spec = pl.BlockSpec((1, D_MODEL, TN), lambda j, i: (layer, 0, j + blk0), pipeline_mode=pl.Buffered(1))
    in_specs, args, aliases = [a_spec, w_spec], [h, w_in], {}
    if p_cache is not None:
        in_specs += [_ANY, _ANY]
        args += [p_cache, s_cache]
        aliases = {2: 1, 3: 2}

    def p_map(j, i):
        return (layer, jnp.minimum(i, P_TILES - 1) // TILES_PER_SEQ, 0, j)

    return pl.pallas_call(
        _inproj_kv_kernel,
        grid=(WIDTH_ATT // TN, ROWS // TM),
        in_specs=in_specs,
        out_specs=[pl.BlockSpec((TM, TN), lambda j, i: (i, j)),
                   pl.BlockSpec((1, 1, N_KEEP, TN), p_map),
                   pl.BlockSpec((1, ROWS_S, TN), lambda j, i: (layer, 0, j))],
        out_shape=[jax.ShapeDtypeStruct((ROWS, WIDTH_ATT), BF16),
                   jax.ShapeDtypeStruct((DEPTH, BATCH, N_KEEP, WIDTH_ATT), F32),
                   jax.ShapeDtypeStruct((DEPTH, ROWS_S, WIDTH_ATT), F32)],
        scratch_shapes=[pltpu.VMEM((D_MODEL, TN), BF16)],
        input_output_aliases=aliases,
        compiler_params=_params("arbitrary", "arbitrary"),
        name="inproj_kv",
    )(*args)


def _inproj_rest_kernel(a_ref, w_ref, wg_ref, wu_ref, wd_ref, z_ref, wgu_out, wd_out, wb_ref):
    _stationary_weights(w_ref, wb_ref)
    z_ref[...] = jnp.dot(a_ref[...], wb_ref[...], preferred_element_type=F32).astype(BF16)
    step = pl.program_id(0) * pl.num_programs(1) + pl.program_id(1)

    @pl.when(step < N_CAST_GU)
    def _():
        for f in range(N_F):
            c0 = f * TF
            valid = min(TF, D_FF - c0)
            for src_ref, lane0 in ((wg_ref, 0), (wu_ref, TF)):
                wgu_out[f, :, lane0:lane0 + valid] = src_ref[0, :, c0:c0 + valid].astype(BF16)
                if valid < TF:
                    wgu_out[f, :, lane0 + valid:lane0 + TF] = jnp.zeros((CAST_GU_ROWS, TF - valid), BF16)

    @pl.when(step < N_CAST_D)
    def _():
        wd_out[...] = wd_ref[0].astype(BF16)

    @pl.when((step >= N_CAST_D) & (step < N_CAST_D_PAD))
    def _():
        wd_out[...] = jnp.zeros(wd_out.shape, BF16)


def _inproj_rest(h, w_in, w_gate, w_up, w_down, layer):
    n_q = WIDTH_ATT // TN_REST
    n_skip = 2 * WIDTH_ATT // TN_REST
    width = D_IN - 2 * WIDTH_ATT
    n_i = ROWS // TM_REST
    assert max(N_CAST_GU, N_CAST_D_PAD) <= (width // TN_REST) * n_i
    gu_chunk = lambda j, i: jnp.minimum(j * n_i + i, N_CAST_GU - 1)
    d_chunk_in = lambda j, i: jnp.minimum(j * n_i + i, N_CAST_D - 1)
    d_chunk = lambda j, i: jnp.minimum(j * n_i + i, N_CAST_D_PAD - 1)
    gu_in = pl.BlockSpec((1, CAST_GU_ROWS, D_FF), lambda j, i: (layer, gu_chunk(j, i), 0))
    return pl.pallas_call(
        _inproj_rest_kernel,
        grid=(width // TN_REST, n_i),
        in_specs=[pl.BlockSpec((TM_REST, D_MODEL), lambda j, i: (i, 0)),
                  pl.BlockSpec((1, D_MODEL, TN_REST),
                               lambda j, i: (layer, 0, jnp.where(j < n_q, j, j + n_skip))),
                  gu_in, gu_in,
                  pl.BlockSpec((1, CAST_D_ROWS, D_MODEL), lambda j, i: (layer, d_chunk_in(j, i), 0))],
        out_specs=[pl.BlockSpec((TM_REST, TN_REST), lambda j, i: (i, j)),
                   pl.BlockSpec((N_F, CAST_GU_ROWS, 2 * TF), lambda j, i: (0, gu_chunk(j, i), 0)),
                   pl.BlockSpec((CAST_D_ROWS, D_MODEL), lambda j, i: (d_chunk(j, i), 0))],
        out_shape=[jax.ShapeDtypeStruct((ROWS, width), BF16),
                   jax.ShapeDtypeStruct((N_F, D_MODEL, 2 * TF), BF16),
                   jax.ShapeDtypeStruct((D_FF_PAD, D_MODEL), BF16)],
        scratch_shapes=[pltpu.VMEM((D_MODEL, TN_REST), BF16)],
        compiler_params=_params("arbitrary", "arbitrary"),
        name="inproj_rest",
    )(h, w_in, w_gate, w_up, w_down)


def _softmax_pv(parts, vals):
    m = parts[0].max(axis=-1, keepdims=True)
    for s in parts[1:]:
        m = jnp.maximum(m, s.max(axis=-1, keepdims=True))
    acc = None
    l = None
    for s, v in zip(parts, vals):
        p = jnp.exp(s - m)
        ls = p.sum(axis=-1, keepdims=True)
        o = jnp.dot(p.astype(BF16), v, preferred_element_type=F32)
        acc = o if acc is None else acc + o
        l = ls if l is None else l + ls
    return acc / l


def _qk(q, k):
    return lax.dot_general(q, k, (((1,), (1,)), ((), ())), preferred_element_type=F32)


def _attn_prompt_kernel(q_ref, *refs):
    k_refs = refs[:ATT_KB]
    v_refs = refs[ATT_KB:2 * ATT_KB]
    bias_ref = refs[2 * ATT_KB]
    o_ref = refs[2 * ATT_KB + 1]
    scale = HEAD_DIM ** -0.5
    for h in range(N_HEADS):
        hs = slice(h * HEAD_DIM, (h + 1) * HEAD_DIM)
        k_all = jnp.concatenate([r[:, hs] for r in k_refs], axis=0)
        v_all = jnp.concatenate([r[:, hs] for r in v_refs], axis=0)
        s = _qk(q_ref[:, hs], k_all) * scale + bias_ref[0, h]
        o_ref[:, hs] = _softmax_pv([s], [v_all]).astype(o_ref.dtype)


def _attn_prompt(z_rest, z_k, z_v, bias):
    blocks_per_seq = SEQ // ATT_Q

    def kv_map(d):
        def index(n):
            first = (n // blocks_per_seq) * blocks_per_seq
            return (jnp.maximum(n - (ATT_KB - 1) + d, first), 0)
        return index

    blk = (ATT_Q, WIDTH_ATT)
    in_specs = [pl.BlockSpec(blk, lambda n: (n, 0))]
    in_specs += [pl.BlockSpec(blk, kv_map(d)) for d in range(ATT_KB)]
    in_specs += [pl.BlockSpec(blk, kv_map(d)) for d in range(ATT_KB)]
    in_specs += [pl.BlockSpec((1, N_HEADS, ATT_Q, ATT_KB * ATT_Q),
                              lambda n: (jnp.minimum(n % blocks_per_seq, ATT_KB - 1), 0, 0, 0))]
    return pl.pallas_call(
        _attn_prompt_kernel,
        grid=(ROWS_P // ATT_Q,),
        in_specs=in_specs,
        out_specs=pl.BlockSpec(blk, lambda n: (n, 0)),
        out_shape=jax.ShapeDtypeStruct((ROWS, WIDTH_ATT), BF16),
        compiler_params=_params("parallel"),
        name="attn_prompt",
    )(z_rest, *([z_k] * ATT_KB), *([z_v] * ATT_KB), bias)


def _attn_sample_kernel(q_ref, kn_ref, vn_ref, ck_ref, cv_ref, bias_c_ref, bias_n_ref, y_hbm, o_ref,
                        kb_scr, vb_scr):
    del y_hbm
    scale = HEAD_DIM ** -0.5
    kb_scr[...] = pltpu.einshape("phd->hpd", ck_ref[0, 0]).astype(BF16)
    vb_scr[...] = pltpu.einshape("phd->hpd", cv_ref[0, 0]).astype(BF16)
    for h in range(N_HEADS):
        hs = slice(h * HEAD_DIM, (h + 1) * HEAD_DIM)
        q = q_ref[:, hs]
        s_c = _qk(q, kb_scr[h]) * scale + bias_c_ref[h]
        s_n = _qk(q, kn_ref[:, hs]) * scale + bias_n_ref[h]
        o = _softmax_pv([s_c, s_n], [vb_scr[h], vn_ref[:, hs]])
        o_ref[:, hs] = o.astype(o_ref.dtype)


def _attn_sample(z_rest, z_k, z_v, cache_k, cache_v, layer, bias_c, bias_n, y_att):
    blk0 = ROWS_P // DEC_SEQ
    n_cache = cache_k.shape[2]
    blk = (DEC_SEQ, WIDTH_ATT)
    rows = pl.BlockSpec(blk, lambda b: (b + blk0, 0))
    cache_spec = pl.BlockSpec((1, 1, n_cache, N_HEADS, HEAD_DIM), lambda b: (layer, b, 0, 0, 0))
    return pl.pallas_call(
        _attn_sample_kernel,
        grid=(DEC_BATCH,),
        in_specs=[rows, rows, rows, cache_spec, cache_spec,
                  pl.BlockSpec((N_HEADS, DEC_SEQ, n_cache), lambda b: (0, 0, 0)),
                  pl.BlockSpec((N_HEADS, DEC_SEQ, DEC_SEQ), lambda b: (0, 0, 0)),
                  _ANY],
        out_specs=rows,
        out_shape=jax.ShapeDtypeStruct((ROWS, WIDTH_ATT), BF16),
        scratch_shapes=[pltpu.VMEM((N_HEADS, n_cache, HEAD_DIM), BF16),
                        pltpu.VMEM((N_HEADS, n_cache, HEAD_DIM), BF16)],
        input_output_aliases={7: 0},
        compiler_params=_params("parallel"),
        name="attn_sample",
    )(z_rest, z_k, z_v, cache_k, cache_v, bias_c, bias_n, y_att)


def _mixb_body(sb_ref, sc_ref, sh_ref, w_ref, o_ref, hist_out_ref, scr, rows):
    scr[HB_PAD:, :] = sc_ref[...].astype(F32) * sh_ref[...].astype(F32)
    for r0 in range(0, rows, CONV_CHUNK):
        acc = None
        for t in range(SC_WIDTH):
            start = r0 + HB_PAD - (SC_WIDTH - 1) + t
            term = scr[start:start + CONV_CHUNK, :] * w_ref[0, t:t + 1, :]
            acc = term if acc is None else acc + term
        y = sb_ref[r0:r0 + CONV_CHUNK, :].astype(F32) * acc
        o_ref[r0:r0 + CONV_CHUNK, :] = y.astype(o_ref.dtype)
    hist_out_ref[0] = scr[rows:rows + HB_PAD, :]


def _mixb_prompt_kernel(sb_ref, sc_ref, sh_ref, scp_ref, shp_ref, w_ref, o_ref, hist_out_ref, scr):
    first = pl.program_id(0) % (SEQ // MIX_ROWS) == 0
    prev = scp_ref[...].astype(F32) * shp_ref[...].astype(F32)
    prev = jnp.where(first, 0.0, prev)
    scr[0:HB_PAD, :] = prev[prev.shape[0] - HB_PAD:, :]
    _mixb_body(sb_ref, sc_ref, sh_ref, w_ref, o_ref, hist_out_ref, scr, MIX_ROWS)


def _mixb_sample_kernel(sb_ref, sc_ref, sh_ref, hist_ref, w_ref, y_hbm, o_ref, hist_out_ref, scr):
    del y_hbm
    scr[0:HB_PAD, :] = hist_ref[0]
    _mixb_body(sb_ref, sc_ref, sh_ref, w_ref, o_ref, hist_out_ref, scr, DEC_SEQ)


def _layer_rows(n, width, layer):
    return pl.BlockSpec((1, n, width), lambda i: (layer, 0, 0))


def _mixb_prompt(z_rest, w, layer):
    halo = 16
    tiles_per_seq = SEQ // MIX_ROWS
    cur = lambda c: pl.BlockSpec((MIX_ROWS, WIDTH_SC), lambda i: (i, c))
    prev = lambda c: pl.BlockSpec((halo, WIDTH_SC),
                                  lambda i: (jnp.maximum(i * (MIX_ROWS // halo) - 1, 0), c))
    return pl.pallas_call(
        _mixb_prompt_kernel,
        grid=(ROWS_P // MIX_ROWS,),
        in_specs=[cur(ZR_SB), cur(ZR_SC), cur(ZR_SH), prev(ZR_SC), prev(ZR_SH),
                  _layer_rows(SC_WIDTH, WIDTH_SC, layer)],
        out_specs=[pl.BlockSpec((MIX_ROWS, WIDTH_SC), lambda i: (i, 0)),
                   pl.BlockSpec((1, HB_PAD, WIDTH_SC), lambda i: (i // tiles_per_seq, 0, 0))],
        out_shape=[jax.ShapeDtypeStruct((ROWS, WIDTH_SC), BF16),
                   jax.ShapeDtypeStruct((BATCH, HB_PAD, WIDTH_SC), F32)],
        scratch_shapes=[pltpu.VMEM((HB_PAD + MIX_ROWS, WIDTH_SC), F32)],
        compiler_params=_params("arbitrary"),
        name="mixb_prompt",
    )(z_rest, z_rest, z_rest, z_rest, z_rest, w)


def _mixb_sample(z_rest, hist, w, layer, y_sc):
    blk0 = ROWS_P // DEC_SEQ
    cur = lambda c: pl.BlockSpec((DEC_SEQ, WIDTH_SC), lambda b: (b + blk0, c))
    return pl.pallas_call(
        _mixb_sample_kernel,
        grid=(DEC_BATCH,),
        in_specs=[cur(ZR_SB), cur(ZR_SC), cur(ZR_SH),
                  pl.BlockSpec((1, HB_PAD, WIDTH_SC), lambda b: (b, 0, 0)),
                  _layer_rows(SC_WIDTH, WIDTH_SC, layer),
                  _ANY],
        out_specs=[pl.BlockSpec((DEC_SEQ, WIDTH_SC), lambda b: (b + blk0, 0)),
                   pl.BlockSpec((1, HB_PAD, WIDTH_SC), lambda b: (b, 0, 0))],
        out_shape=[jax.ShapeDtypeStruct((ROWS, WIDTH_SC), BF16),
                   jax.ShapeDtypeStruct((DEC_BATCH, HB_PAD, WIDTH_SC), F32)],
        scratch_shapes=[pltpu.VMEM((HB_PAD + DEC_SEQ, WIDTH_SC), F32)],
        input_output_aliases={5: 0},
        compiler_params=_params("arbitrary"),
        name="mixb_sample",
    )(z_rest, z_rest, z_rest, hist, w, y_sc)


def _glu(a, g):
    return a.astype(F32) * jax.nn.sigmoid(g.astype(F32))


def _mixc_body(ca_ref, cg_ref, w_ref, cb_ref, lg_ref, lb_ref, o_ref, hist_out_ref, scr, rows):
    scr[HC_PAD:, :] = _glu(ca_ref[...], cg_ref[...])
    win_rows = CONV_CHUNK + HC_PAD
    lead = HC_PAD - (CV_WIDTH - 1)
    for r0 in range(0, rows, CONV_CHUNK):
        cols = []
        for c0 in range(0, WIDTH_CV, LANES):
            cs = slice(c0, c0 + LANES)
            win = scr[r0:r0 + win_rows, cs]
            acc = None
            for phase in range(SUBLANES):
                shifted = win if phase == 0 else pltpu.roll(win, win_rows - phase, axis=0)
                for base in range(0, win_rows - CONV_CHUNK + 1, SUBLANES):
                    t = base + phase - lead
                    if 0 <= t < CV_WIDTH:
                        term = shifted[base:base + CONV_CHUNK, :] * w_ref[0, t:t + 1, cs]
                        acc = term if acc is None else acc + term
            cols.append(acc)
        zc = jnp.concatenate(cols, axis=1) + cb_ref[0]
        mu = jnp.mean(zc, axis=-1, keepdims=True)
        cen = zc - mu
        var = jnp.mean(cen * cen, axis=-1, keepdims=True)
        y = cen * lax.rsqrt(var + EPS) * lg_ref[0] + lb_ref[0]
        o_ref[r0:r0 + CONV_CHUNK, :] = (y * jax.nn.sigmoid(y)).astype(o_ref.dtype)
    hist_out_ref[0] = scr[rows:rows + HC_PAD, :]


def _mixc_prompt_kernel(ca_ref, cg_ref, cap_ref, cgp_ref, w_ref, cb_ref, lg_ref, lb_ref,
                        o_ref, hist_out_ref, scr):
    first = pl.program_id(0) % (SEQ // MIX_ROWS) == 0
    scr[0:HC_PAD, :] = jnp.where(first, 0.0, _glu(cap_ref[...], cgp_ref[...]))
    _mixc_body(ca_ref, cg_ref, w_ref, cb_ref, lg_ref, lb_ref, o_ref, hist_out_ref, scr, MIX_ROWS)


def _mixc_sample_kernel(ca_ref, cg_ref, hist_ref, w_ref, cb_ref, lg_ref, lb_ref, y_hbm,
                        o_ref, hist_out_ref, scr):
    del y_hbm
    scr[0:HC_PAD, :] = hist_ref[0]
    _mixc_body(ca_ref, cg_ref, w_ref, cb_ref, lg_ref, lb_ref, o_ref, hist_out_ref, scr, DEC_SEQ)


def _mixc_prompt(z_rest, w, cb, lg, lb, layer):
    tiles_per_seq = SEQ // MIX_ROWS
    cur = lambda c: pl.BlockSpec((MIX_ROWS, WIDTH_CV), lambda i: (i, c))
    prev = lambda c: pl.BlockSpec((HC_PAD, WIDTH_CV),
                                  lambda i: (jnp.maximum(i * (MIX_ROWS // HC_PAD) - 1, 0), c))
    vec = _layer_rows(1, WIDTH_CV, layer)
    return pl.pallas_call(
        _mixc_prompt_kernel,
        grid=(ROWS_P // MIX_ROWS,),
        in_specs=[cur(ZR_CA), cur(ZR_CG), prev(ZR_CA), prev(ZR_CG),
                  _layer_rows(CV_WIDTH, WIDTH_CV, layer), vec, vec, vec],
        out_specs=[pl.BlockSpec((MIX_ROWS, WIDTH_CV), lambda i: (i, 0)),
                   pl.BlockSpec((1, HC_PAD, WIDTH_CV), lambda i: (i // tiles_per_seq, 0, 0))],
        out_shape=[jax.ShapeDtypeStruct((ROWS, WIDTH_CV), BF16),
                   jax.ShapeDtypeStruct((BATCH, HC_PAD, WIDTH_CV), F32)],
        scratch_shapes=[pltpu.VMEM((HC_PAD + MIX_ROWS, WIDTH_CV), F32)],
        compiler_params=_params("arbitrary"),
        name="mixc_prompt",
    )(z_rest, z_rest, z_rest, z_rest, w, cb, lg, lb)


def _mixc_sample(z_rest, hist, w, cb, lg, lb, layer, y_cv):
    blk0 = ROWS_P // DEC_SEQ
    cur = lambda c: pl.BlockSpec((DEC_SEQ, WIDTH_CV), lambda b: (b + blk0, c))
    vec = _layer_rows(1, WIDTH_CV, layer)
    return pl.pallas_call(
        _mixc_sample_kernel,
        grid=(DEC_BATCH,),
        in_specs=[cur(ZR_CA), cur(ZR_CG),
                  pl.BlockSpec((1, HC_PAD, WIDTH_CV), lambda b: (b, 0, 0)),
                  _layer_rows(CV_WIDTH, WIDTH_CV, layer), vec, vec, vec,
                  _ANY],
        out_specs=[pl.BlockSpec((DEC_SEQ, WIDTH_CV), lambda b: (b + blk0, 0)),
                   pl.BlockSpec((1, HC_PAD, WIDTH_CV), lambda b: (b, 0, 0))],
        out_shape=[jax.ShapeDtypeStruct((ROWS, WIDTH_CV), BF16),
                   jax.ShapeDtypeStruct((DEC_BATCH, HC_PAD, WIDTH_CV), F32)],
        scratch_shapes=[pltpu.VMEM((HC_PAD + DEC_SEQ, WIDTH_CV), F32)],
        input_output_aliases={7: 0},
        compiler_params=_params("arbitrary"),
        name="mixc_sample",
    )(z_rest, z_rest, hist, w, cb, lg, lb, y_cv)


def _outproj_kernel(ya_ref, yb_ref, yc_ref, w_ref, xp_ref, xs_ref, o_ref, wb_ref):
    _stationary_weights(w_ref, wb_ref)
    a0, a1 = WIDTH_ATT, WIDTH_ATT + WIDTH_SC
    acc = jnp.dot(ya_ref[...], wb_ref[:a0, :], preferred_element_type=F32)
    acc += jnp.dot(yb_ref[...], wb_ref[a0:a1, :], preferred_element_type=F32)
    acc += jnp.dot(yc_ref[...], wb_ref[a1:, :], preferred_element_type=F32)
    x = jnp.where(pl.program_id(1) < P_TILES, xp_ref[...], xs_ref[...])
    o_ref[...] = x + acc


def _outproj(y_att, y_sc, y_cv, w_out, layer, x_p, x_s, s_blk):
    row = lambda width: pl.BlockSpec((TM, width), lambda j, i: (i, 0))
    return pl.pallas_call(
        _outproj_kernel,
        grid=(D_MODEL // TN, ROWS // TM),
        in_specs=[row(WIDTH_ATT), row(WIDTH_SC), row(WIDTH_CV),
                  pl.BlockSpec((1, D_MODEL, TN), lambda j, i: (layer, 0, j), pipeline_mode=pl.Buffered(1)),
                  pl.BlockSpec((TM, TN), lambda j, i: (jnp.minimum(i, P_TILES - 1), j)),
                  pl.BlockSpec((TM, TN), lambda j, i: (s_blk, j))],
        out_specs=pl.BlockSpec((TM, TN), lambda j, i: (i, j)),
        out_shape=jax.ShapeDtypeStruct((ROWS, D_MODEL), F32),
        scratch_shapes=[pltpu.VMEM((D_MODEL, TN), BF16)],
        compiler_params=_params("arbitrary", "arbitrary"),
        name="outproj",
    )(y_att, y_sc, y_cv, w_out, x_p, x_s)


def _rmsnorm_rows(src_ref, g_ref, dst_ref, copy_ref=None):
    col_blocks = [slice(c0, c0 + NORM_COLS) for c0 in range(0, D_MODEL, NORM_COLS)]

    def chunk(c, carry):
        rows = pl.ds(pl.multiple_of(c * NORM_CHUNK, NORM_CHUNK), NORM_CHUNK)
        ss = None
        for cs in col_blocks:
            x = src_ref[rows, cs]
            ss = x * x if ss is None else ss + x * x
        inv = lax.rsqrt(jnp.sum(ss, axis=-1, keepdims=True) * (1.0 / D_MODEL) + EPS)
        for cs in col_blocks:
            x = src_ref[rows, cs]
            dst_ref[rows, cs] = (x * inv * g_ref[0, :, cs]).astype(dst_ref.dtype)
            if copy_ref is not None:
                copy_ref[rows, cs] = x
        return carry

    lax.fori_loop(0, TM // NORM_CHUNK, chunk, 0, unroll=4)


def _ffn_kernel(x_ref, g_ref, wgu_ref, wd_ref, o_ref, h_scr):
    @pl.when(pl.program_id(1) == 0)
    def _():
        _rmsnorm_rows(x_ref, g_ref, h_scr, copy_ref=o_ref)

    gu = jnp.dot(h_scr[...], wgu_ref[0], preferred_element_type=F32)
    g = gu[:, :TF]
    u = gu[:, TF:]
    act = (g * jax.nn.sigmoid(g) * u).astype(BF16)
    o_ref[...] += jnp.dot(act, wd_ref[...], preferred_element_type=F32)


def _ffn(x, g, w_gu, w_down_b, layer):
    return pl.pallas_call(
        _ffn_kernel,
        grid=(ROWS // TM, N_F),
        in_specs=[pl.BlockSpec((TM, D_MODEL), lambda i, f: (i, 0), pipeline_mode=pl.Buffered(1)),
                  pl.BlockSpec((1, 1, D_MODEL), lambda i, f: (layer, 0, 0)),
                  pl.BlockSpec((1, D_MODEL, 2 * TF), lambda i, f: (f, 0, 0)),
                  pl.BlockSpec((TF, D_MODEL), lambda i, f: (f, 0))],
        out_specs=pl.BlockSpec((TM, D_MODEL), lambda i, f: (i, 0)),
        out_shape=jax.ShapeDtypeStruct((ROWS, D_MODEL), F32),
        scratch_shapes=[pltpu.VMEM((TM, D_MODEL), BF16)],
        compiler_params=_params("parallel", "arbitrary", vmem=VMEM_LIMIT_FFN),
        name="ffn",
    )(x, g, w_gu, w_down_b)


def _rel_bias_tile(rel_bias_l, rows, cols, n_past):
    n_heads = rel_bias_l.shape[0]
    period = rows + cols
    k = jnp.arange(period)
    dist = n_past - jnp.where(k < cols, k, k - period)
    u = rel_bias_l[:, jnp.clip(dist, -MAX_REL, MAX_REL) + MAX_REL].astype(F32)
    flat = jnp.broadcast_to(u[:, None, :], (n_heads, rows, period)).reshape(n_heads, rows * period)
    return flat[:, :rows * (period - 1)].reshape(n_heads, rows, period - 1)[:, :, :cols]


def _prompt_bias(rel_bias_l):
    bias = _rel_bias_tile(rel_bias_l, ATT_Q, ATT_KB * ATT_Q, ATT_PAST)
    r = jnp.arange(ATT_Q)[:, None]
    c = jnp.arange(ATT_KB * ATT_Q)[None, :]
    q_chunk = r // CHUNK + LEFT_CHUNKS
    k_chunk = c // CHUNK
    band = (k_chunk <= q_chunk) & (k_chunk >= q_chunk - LEFT_CHUNKS)
    variant = jnp.arange(ATT_KB)[:, None, None]
    valid = band[None] & (c[None] // ATT_Q >= ATT_KB - 1 - variant)
    return jnp.where(valid[:, None], bias[None], NEG)


def _pad_hist(hist, pad_to):
    return jnp.pad(hist, ((0, 0), (pad_to - hist.shape[1], 0), (0, 0)))


def kernel(x_prompt, x_sample, cache_attn_k, cache_attn_v, cache_conv_b, cache_conv_c, norm_mix_g, w_in,
           rel_bias, conv_b_w, conv_c_w, conv_c_b, ln_c_g, ln_c_b, w_out, norm_ffn_g, w_ffn_gate, w_ffn_up,
           w_ffn_down, final_norm_g):
    n_cache = cache_attn_k.shape[2]
    x_p = x_prompt.reshape(ROWS_P, D_MODEL)
    x_s = x_sample.reshape(ROWS_S, D_MODEL)
    g_ffn = norm_ffn_g.reshape(DEPTH, 1, D_MODEL)
    cb, lg, lb = (a.reshape(DEPTH, 1, WIDTH_CV) for a in (conv_c_b, ln_c_g, ln_c_b))

    hists = {k: [] for k in ("pb", "pc", "sb", "sc")}
    pk = pv = sk = sv = None
    x = None
    h = _rmsnorm(x_p, norm_mix_g[0], BF16, 0, ROWS_P, ROWS)
    h = _rmsnorm(x_s, norm_mix_g[0], BF16, 0, ROWS_S, ROWS, dst_row0=ROWS_P, dst=h)
    for l in range(DEPTH):
        z_k, pk, sk = _inproj_kv(h, w_in, l, WIDTH_ATT, pk, sk)
        z_v, pv, sv = _inproj_kv(h, w_in, l, 2 * WIDTH_ATT, pv, sv)
        z_rest, w_gu, w_down_b = _inproj_rest(h, w_in, w_ffn_gate, w_ffn_up, w_ffn_down, l)

        y_att = _attn_prompt(z_rest, z_k, z_v, _prompt_bias(rel_bias[l]))
        bias_s = _rel_bias_tile(rel_bias[l], DEC_SEQ, n_cache + DEC_SEQ, n_cache)
        y_att = _attn_sample(z_rest, z_k, z_v, cache_attn_k, cache_attn_v, l,
                             bias_s[:, :, :n_cache], bias_s[:, :, n_cache:], y_att)

        y_sc, hb_p = _mixb_prompt(z_rest, conv_b_w, l)
        y_sc, hb_s = _mixb_sample(z_rest, _pad_hist(cache_conv_b[l], HB_PAD), conv_b_w, l, y_sc)
        y_cv, hc_p = _mixc_prompt(z_rest, conv_c_w, cb, lg, lb, l)
        y_cv, hc_s = _mixc_sample(z_rest, _pad_hist(cache_conv_c[l], HC_PAD), conv_c_w, cb, lg, lb, l, y_cv)

        if l == 0:
            x = _outproj(y_att, y_sc, y_cv, w_out, l, x_p, x_s, 0)
        else:
            x = _outproj(y_att, y_sc, y_cv, w_out, l, x, x, P_TILES)
        x = _ffn(x, g_ffn, w_gu, w_down_b, l)
        if l + 1 < DEPTH:
            h = _rmsnorm(x, norm_mix_g[l + 1], BF16, 0, ROWS, ROWS)

        hists["pb"].append(hb_p[:, HB_PAD - (SC_WIDTH - 1):])
        hists["pc"].append(hc_p[:, HC_PAD - (CV_WIDTH - 1):])
        hists["sb"].append(hb_s[:, HB_PAD - (SC_WIDTH - 1):])
        hists["sc"].append(hc_s[:, HC_PAD - (CV_WIDTH - 1):])

    y_prompt = _rmsnorm(x, final_norm_g, F32, 0, ROWS_P, ROWS_P).reshape(BATCH, SEQ, D_MODEL)
    y_sample = _rmsnorm(x, final_norm_g, F32, ROWS_P, ROWS_S, ROWS_S).reshape(DEC_BATCH, DEC_SEQ, D_MODEL)
    st = {k: jnp.stack(v) for k, v in hists.items()}
    heads_p = (DEPTH, BATCH, N_KEEP, N_HEADS, HEAD_DIM)
    heads_s = (DEPTH, DEC_BATCH, DEC_SEQ, N_HEADS, HEAD_DIM)
    return (y_prompt, y_sample, pk.reshape(heads_p), pv.reshape(heads_p), st["pb"], st["pc"],
            sk.reshape(heads_s), sv.reshape(heads_s), st["sb"], st["sc"])
```

```python
import jax
import jax.numpy as jnp
from jax import lax
from jax.experimental import pallas as pl
from jax.experimental.pallas import tpu as pltpu

F32 = jnp.float32
BF16 = jnp.bfloat16

D_MODEL = 4096
BATCH = 8
SEQ = 2048
DEPTH = 2
DEC_BATCH = 16
DEC_SEQ = 32
CHUNK = 64
LEFT_CHUNKS = 8
ATT_PAST = LEFT_CHUNKS * CHUNK
WIDTH_ATT = 2048
WIDTH_SC = 1024
WIDTH_CV = 1024
HEAD_DIM = 128
N_HEADS = 16
MAX_REL = 256
SC_WIDTH = 3
CV_WIDTH = 31
D_FF = 11008
D_IN = 11264
EPS = 1e-6

ROWS_P = BATCH * SEQ
ROWS_S = DEC_BATCH * DEC_SEQ
ROWS = ROWS_P + ROWS_S
N_KEEP = min(ATT_PAST, SEQ)

TM = 512
TN = 1024
TM_REST = ROWS // 16
TN_REST = 512
TF = 512
N_F = -(-D_FF // TF)
D_FF_PAD = N_F * TF
CAST_GU_ROWS = 32
CAST_D_ROWS = 64
N_CAST_GU = D_MODEL // CAST_GU_ROWS
N_CAST_D = D_FF // CAST_D_ROWS
N_CAST_D_PAD = D_FF_PAD // CAST_D_ROWS
NORM_ROWS = 512
NORM_CHUNK = 16
NORM_COLS = 512
ATT_Q = 256
ATT_KB = ATT_PAST // ATT_Q + 1
MIX_ROWS = 512
CONV_CHUNK = 32
HB_PAD = 8
HC_PAD = 32
NEG = -1e30
LANES = 128
SUBLANES = 8
VMEM_LIMIT = 56 * 1024 * 1024
VMEM_LIMIT_FFN = 60 * 1024 * 1024

P_TILES = ROWS_P // TM
TILES_PER_SEQ = SEQ // TM

ZR_SB, ZR_SC, ZR_SH, ZR_CA, ZR_CG = 2, 3, 4, 5, 6

assert TM == N_KEEP == ROWS_S and ROWS == (P_TILES + 1) * TM


def _params(*sem, vmem=VMEM_LIMIT):
    return pltpu.CompilerParams(dimension_semantics=sem, vmem_limit_bytes=vmem)


_ANY = pl.BlockSpec(memory_space=pl.ANY)


def _rmsnorm_kernel(x_ref, g_ref, *rest):
    o_ref = rest[-1]
    x = x_ref[...]
    ms = jnp.mean(x * x, axis=-1, keepdims=True)
    o_ref[...] = (x * lax.rsqrt(ms + EPS) * g_ref[...]).astype(o_ref.dtype)


def _rmsnorm(x, g, out_dtype, src_row0, rows, out_rows, dst_row0=0, dst=None):
    src0, dst0 = src_row0 // NORM_ROWS, dst_row0 // NORM_ROWS
    args = [x, g.reshape(1, D_MODEL)]
    in_specs = [pl.BlockSpec((NORM_ROWS, D_MODEL), lambda i: (i + src0, 0)),
                pl.BlockSpec((1, D_MODEL), lambda i: (0, 0))]
    aliases = {}
    if dst is not None:
        args.append(dst)
        in_specs.append(_ANY)
        aliases = {2: 0}
    return pl.pallas_call(
        _rmsnorm_kernel,
        grid=(rows // NORM_ROWS,),
        in_specs=in_specs,
        out_specs=pl.BlockSpec((NORM_ROWS, D_MODEL), lambda i: (i + dst0, 0)),
        out_shape=jax.ShapeDtypeStruct((out_rows, D_MODEL), out_dtype),
        input_output_aliases=aliases,
        compiler_params=_params("parallel"),
        name="rmsnorm",
    )(*args)


def _stationary_weights(w_ref, wb_ref):
    @pl.when(pl.program_id(1) == 0)
    def _():
        wb_ref[...] = w_ref[0].astype(BF16)


def _inproj_kv_kernel(a_ref, w_ref, *rest):
    z_ref, pc_ref, sc_ref, wb_ref = rest[-4:]
    _stationary_weights(w_ref, wb_ref)
    i = pl.program_id(1)
    r = jnp.dot(a_ref[...], wb_ref[...], preferred_element_type=F32)
    z_ref[...] = r.astype(BF16)

    @pl.when((i < P_TILES) & (i % TILES_PER_SEQ == TILES_PER_SEQ - 1))
    def _():
        pc_ref[0, 0] = r

    @pl.when(i == P_TILES)
    def _():
        sc_ref[0] = r


def _inproj_kv(h, w_in, layer, col0, p_cache, s_cache):
    blk0 = col0 // TN
    a_spec = pl.BlockSpec((TM, D_MODEL), lambda j, i: (i, 0))
    w_spec = pl.BlockSpec((1, D_MODEL, TN), lambda j, i: (layer, 0, j + blk0), pipeline_mode=pl.Buffered(1))
    in_specs, args, aliases = [a_spec, w_spec], [h, w_in], {}
    if p_cache is not None:
        in_specs += [_ANY, _ANY]
        args += [p_cache, s_cache]
        aliases = {2: 1, 3: 2}

    def p_map(j, i):
        return (layer, jnp.minimum(i, P_TILES - 1) // TILES_PER_SEQ, 0, j)

    return pl.pallas_call(
        _inproj_kv_kernel,
        grid=(WIDTH_ATT // TN, ROWS // TM),
        in_specs=in_specs,
        out_specs=[pl.BlockSpec((TM, TN), lambda j, i: (i, j)),
                   pl.BlockSpec((1, 1, N_KEEP, TN), p_map),
                   pl.BlockSpec((1, ROWS_S, TN), lambda j, i: (layer, 0, j))],
        out_shape=[jax.ShapeDtypeStruct((ROWS, WIDTH_ATT), BF16),
                   jax.ShapeDtypeStruct((DEPTH, BATCH, N_KEEP, WIDTH_ATT), F32),
                   jax.ShapeDtypeStruct((DEPTH, ROWS_S, WIDTH_ATT), F32)],
        scratch_shapes=[pltpu.VMEM((D_MODEL, TN), BF16)],
        input_output_aliases=aliases,
        compiler_params=_params("arbitrary", "arbitrary"),
        name="inproj_kv",
    )(*args)


def _inproj_rest_kernel(a_ref, w_ref, wg_ref, wu_ref, wd_ref, z_ref, wgu_out, wd_out, wb_ref):
    _stationary_weights(w_ref, wb_ref)
    z_ref[...] = jnp.dot(a_ref[...], wb_ref[...], preferred_element_type=F32).astype(BF16)
    step = pl.program_id(0) * pl.num_programs(1) + pl.program_id(1)

    @pl.when(step < N_CAST_GU)
    def _():
        for f in range(N_F):
            c0 = f * TF
            valid = min(TF, D_FF - c0)
            for src_ref, lane0 in ((wg_ref, 0), (wu_ref, TF)):
                wgu_out[f, :, lane0:lane0 + valid] = src_ref[0, :, c0:c0 + valid].astype(BF16)
                if valid < TF:
                    wgu_out[f, :, lane0 + valid:lane0 + TF] = jnp.zeros((CAST_GU_ROWS, TF - valid), BF16)

    @pl.when(step < N_CAST_D)
    def _():
        wd_out[...] = wd_ref[0].astype(BF16)

    @pl.when((step >= N_CAST_D) & (step < N_CAST_D_PAD))
    def _():
        wd_out[...] = jnp.zeros(wd_out.shape, BF16)


def _inproj_rest(h, w_in, w_gate, w_up, w_down, layer):
    n_q = WIDTH_ATT // TN_REST
    n_skip = 2 * WIDTH_ATT // TN_REST
    width = D_IN - 2 * WIDTH_ATT
    n_i = ROWS // TM_REST
    assert max(N_CAST_GU, N_CAST_D_PAD) <= (width // TN_REST) * n_i
    gu_chunk = lambda j, i: jnp.minimum(j * n_i + i, N_CAST_GU - 1)
    d_chunk_in = lambda j, i: jnp.minimum(j * n_i + i, N_CAST_D - 1)
    d_chunk = lambda j, i: jnp.minimum(j * n_i + i, N_CAST_D_PAD - 1)
    gu_in = pl.BlockSpec((1, CAST_GU_ROWS, D_FF), lambda j, i: (layer, gu_chunk(j, i), 0))
    return pl.pallas_call(
        _inproj_rest_kernel,
        grid=(width // TN_REST, n_i),
        in_specs=[pl.BlockSpec((TM_REST, D_MODEL), lambda j, i: (i, 0)),
                  pl.BlockSpec((1, D_MODEL, TN_REST),
                               lambda j, i: (layer, 0, jnp.where(j < n_q, j, j + n_skip))),
                  gu_in, gu_in,
                  pl.BlockSpec((1, CAST_D_ROWS, D_MODEL), lambda j, i: (layer, d_chunk_in(j, i), 0))],
        out_specs=[pl.BlockSpec((TM_REST, TN_REST), lambda j, i: (i, j)),
                   pl.BlockSpec((N_F, CAST_GU_ROWS, 2 * TF), lambda j, i: (0, gu_chunk(j, i), 0)),
                   pl.BlockSpec((CAST_D_ROWS, D_MODEL), lambda j, i: (d_chunk(j, i), 0))],
        out_shape=[jax.ShapeDtypeStruct((ROWS, width), BF16),
                   jax.ShapeDtypeStruct((N_F, D_MODEL, 2 * TF), BF16),
                   jax.ShapeDtypeStruct((D_FF_PAD, D_MODEL), BF16)],
        scratch_shapes=[pltpu.VMEM((D_MODEL, TN_REST), BF16)],
        compiler_params=_params("arbitrary", "arbitrary"),
        name="inproj_rest",
    )(h, w_in, w_gate, w_up, w_down)


def _softmax_pv(parts, vals):
    m = parts[0].max(axis=-1, keepdims=True)
    for s in parts[1:]:
        m = jnp.maximum(m, s.max(axis=-1, keepdims=True))
    acc = None
    l = None
    for s, v in zip(parts, vals):
        p = jnp.exp(s - m)
        ls = p.sum(axis=-1, keepdims=True)
        o = jnp.dot(p.astype(BF16), v, preferred_element_type=F32)
        acc = o if acc is None else acc + o
        l = ls if l is None else l + ls
    return acc / l


def _qk(q, k):
    return lax.dot_general(q, k, (((1,), (1,)), ((), ())), preferred_element_type=F32)


def _attn_prompt_kernel(q_ref, *refs):
    k_refs = refs[:ATT_KB]
    v_refs = refs[ATT_KB:2 * ATT_KB]
    bias_ref = refs[2 * ATT_KB]
    o_ref = refs[2 * ATT_KB + 1]
    scale = HEAD_DIM ** -0.5
    for h in range(N_HEADS):
        hs = slice(h * HEAD_DIM, (h + 1) * HEAD_DIM)
        k_all = jnp.concatenate([r[:, hs] for r in k_refs], axis=0)
        v_all = jnp.concatenate([r[:, hs] for r in v_refs], axis=0)
        s = _qk(q_ref[:, hs], k_all) * scale + bias_ref[0, h]
        o_ref[:, hs] = _softmax_pv([s], [v_all]).astype(o_ref.dtype)


def _attn_prompt(z_rest, z_k, z_v, bias):
    blocks_per_seq = SEQ // ATT_Q

    def row_block(p, b):
        return b * blocks_per_seq + p

    def kv_map(d):
        def index(p, b):
            return (row_block(jnp.maximum(p - (ATT_KB - 1) + d, 0), b), 0)
        return index

    blk = (ATT_Q, WIDTH_ATT)
    in_specs = [pl.BlockSpec(blk, lambda p, b: (row_block(p, b), 0))]
    in_specs += [pl.BlockSpec(blk, kv_map(d)) for d in range(ATT_KB)]
    in_specs += [pl.BlockSpec(blk, kv_map(d)) for d in range(ATT_KB)]
    in_specs += [pl.BlockSpec((1, N_HEADS, ATT_Q, ATT_KB * ATT_Q),
                              lambda p, b: (jnp.minimum(p, ATT_KB - 1), 0, 0, 0))]
    return pl.pallas_call(
        _attn_prompt_kernel,
        grid=(blocks_per_seq, BATCH),
        in_specs=in_specs,
        out_specs=pl.BlockSpec(blk, lambda p, b: (row_block(p, b), 0)),
        out_shape=jax.ShapeDtypeStruct((ROWS, WIDTH_ATT), BF16),
        compiler_params=_params("parallel", "parallel"),
        name="attn_prompt",
    )(z_rest, *([z_k] * ATT_KB), *([z_v] * ATT_KB), bias)


def _attn_sample_kernel(q_ref, kn_ref, vn_ref, ck_ref, cv_ref, bias_c_ref, bias_n_ref, y_hbm, o_ref,
                        kb_scr, vb_scr):
    del y_hbm
    scale = HEAD_DIM ** -0.5
    kb_scr[...] = pltpu.einshape("phd->hpd", ck_ref[0, 0]).astype(BF16)
    vb_scr[...] = pltpu.einshape("phd->hpd", cv_ref[0, 0]).astype(BF16)
    for h in range(N_HEADS):
        hs = slice(h * HEAD_DIM, (h + 1) * HEAD_DIM)
        q = q_ref[:, hs]
        s_c = _qk(q, kb_scr[h]) * scale + bias_c_ref[h]
        s_n = _qk(q, kn_ref[:, hs]) * scale + bias_n_ref[h]
        o = _softmax_pv([s_c, s_n], [vb_scr[h], vn_ref[:, hs]])
        o_ref[:, hs] = o.astype(o_ref.dtype)


def _attn_sample(z_rest, z_k, z_v, cache_k, cache_v, layer, bias_c, bias_n, y_att):
    blk0 = ROWS_P // DEC_SEQ
    n_cache = cache_k.shape[2]
    blk = (DEC_SEQ, WIDTH_ATT)
    rows = pl.BlockSpec(blk, lambda b: (b + blk0, 0))
    cache_spec = pl.BlockSpec((1, 1, n_cache, N_HEADS, HEAD_DIM), lambda b: (layer, b, 0, 0, 0))
    return pl.pallas_call(
        _attn_sample_kernel,
        grid=(DEC_BATCH,),
        in_specs=[rows, rows, rows, cache_spec, cache_spec,
                  pl.BlockSpec((N_HEADS, DEC_SEQ, n_cache), lambda b: (0, 0, 0)),
                  pl.BlockSpec((N_HEADS, DEC_SEQ, DEC_SEQ), lambda b: (0, 0, 0)),
                  _ANY],
        out_specs=rows,
        out_shape=jax.ShapeDtypeStruct((ROWS, WIDTH_ATT), BF16),
        scratch_shapes=[pltpu.VMEM((N_HEADS, n_cache, HEAD_DIM), BF16),
                        pltpu.VMEM((N_HEADS, n_cache, HEAD_DIM), BF16)],
        input_output_aliases={7: 0},
        compiler_params=_params("parallel"),
        name="attn_sample",
    )(z_rest, z_k, z_v, cache_k, cache_v, bias_c, bias_n, y_att)


def _mixb_body(sb_ref, sc_ref, sh_ref, w_ref, o_ref, hist_out_ref, scr, rows):
    scr[HB_PAD:, :] = sc_ref[...].astype(F32) * sh_ref[...].astype(F32)
    for r0 in range(0, rows, CONV_CHUNK):
        acc = None
        for t in range(SC_WIDTH):
            start = r0 + HB_PAD - (SC_WIDTH - 1) + t
            term = scr[start:start + CONV_CHUNK, :] * w_ref[0, t:t + 1, :]
            acc = term if acc is None else acc + term
        y = sb_ref[r0:r0 + CONV_CHUNK, :].astype(F32) * acc
        o_ref[r0:r0 + CONV_CHUNK, :] = y.astype(o_ref.dtype)
    hist_out_ref[0] = scr[rows:rows + HB_PAD, :]


def _mixb_prompt_kernel(sb_ref, sc_ref, sh_ref, scp_ref, shp_ref, w_ref, o_ref, hist_out_ref, scr):
    first = pl.program_id(0) % (SEQ // MIX_ROWS) == 0
    prev = scp_ref[...].astype(F32) * shp_ref[...].astype(F32)
    prev = jnp.where(first, 0.0, prev)
    scr[0:HB_PAD, :] = prev[prev.shape[0] - HB_PAD:, :]
    _mixb_body(sb_ref, sc_ref, sh_ref, w_ref, o_ref, hist_out_ref, scr, MIX_ROWS)


def _mixb_sample_kernel(sb_ref, sc_ref, sh_ref, hist_ref, w_ref, y_hbm, o_ref, hist_out_ref, scr):
    del y_hbm
    scr[0:HB_PAD, :] = hist_ref[0]
    _mixb_body(sb_ref, sc_ref, sh_ref, w_ref, o_ref, hist_out_ref, scr, DEC_SEQ)


def _layer_rows(n, width, layer):
    return pl.BlockSpec((1, n, width), lambda i: (layer, 0, 0))


def _mixb_prompt(z_rest, w, layer):
    halo = 16
    tiles_per_seq = SEQ // MIX_ROWS
    cur = lambda c: pl.BlockSpec((MIX_ROWS, WIDTH_SC), lambda i: (i, c))
    prev = lambda c: pl.BlockSpec((halo, WIDTH_SC),
                                  lambda i: (jnp.maximum(i * (MIX_ROWS // halo) - 1, 0), c))
    return pl.pallas_call(
        _mixb_prompt_kernel,
        grid=(ROWS_P // MIX_ROWS,),
        in_specs=[cur(ZR_SB), cur(ZR_SC), cur(ZR_SH), prev(ZR_SC), prev(ZR_SH),
                  _layer_rows(SC_WIDTH, WIDTH_SC, layer)],
        out_specs=[pl.BlockSpec((MIX_ROWS, WIDTH_SC), lambda i: (i, 0)),
                   pl.BlockSpec((1, HB_PAD, WIDTH_SC), lambda i: (i // tiles_per_seq, 0, 0))],
        out_shape=[jax.ShapeDtypeStruct((ROWS, WIDTH_SC), BF16),
                   jax.ShapeDtypeStruct((BATCH, HB_PAD, WIDTH_SC), F32)],
        scratch_shapes=[pltpu.VMEM((HB_PAD + MIX_ROWS, WIDTH_SC), F32)],
        compiler_params=_params("arbitrary"),
        name="mixb_prompt",
    )(z_rest, z_rest, z_rest, z_rest, z_rest, w)


def _mixb_sample(z_rest, hist, w, layer, y_sc):
    blk0 = ROWS_P // DEC_SEQ
    cur = lambda c: pl.BlockSpec((DEC_SEQ, WIDTH_SC), lambda b: (b + blk0, c))
    return pl.pallas_call(
        _mixb_sample_kernel,
        grid=(DEC_BATCH,),
        in_specs=[cur(ZR_SB), cur(ZR_SC), cur(ZR_SH),
                  pl.BlockSpec((1, HB_PAD, WIDTH_SC), lambda b: (b, 0, 0)),
                  _layer_rows(SC_WIDTH, WIDTH_SC, layer),
                  _ANY],
        out_specs=[pl.BlockSpec((DEC_SEQ, WIDTH_SC), lambda b: (b + blk0, 0)),
                   pl.BlockSpec((1, HB_PAD, WIDTH_SC), lambda b: (b, 0, 0))],
        out_shape=[jax.ShapeDtypeStruct((ROWS, WIDTH_SC), BF16),
                   jax.ShapeDtypeStruct((DEC_BATCH, HB_PAD, WIDTH_SC), F32)],
        scratch_shapes=[pltpu.VMEM((HB_PAD + DEC_SEQ, WIDTH_SC), F32)],
        input_output_aliases={5: 0},
        compiler_params=_params("arbitrary"),
        name="mixb_sample",
    )(z_rest, z_rest, z_rest, hist, w, y_sc)


def _glu(a, g):
    return a.astype(F32) * jax.nn.sigmoid(g.astype(F32))


def _mixc_body(ca_ref, cg_ref, w_ref, cb_ref, lg_ref, lb_ref, o_ref, hist_out_ref, scr, rows):
    scr[HC_PAD:, :] = _glu(ca_ref[...], cg_ref[...])
    win_rows = CONV_CHUNK + HC_PAD
    lead = HC_PAD - (CV_WIDTH - 1)
    for r0 in range(0, rows, CONV_CHUNK):
        cols = []
        for c0 in range(0, WIDTH_CV, LANES):
            cs = slice(c0, c0 + LANES)
            win = scr[r0:r0 + win_rows, cs]
            acc = None
            for phase in range(SUBLANES):
                shifted = win if phase == 0 else pltpu.roll(win, win_rows - phase, axis=0)
                for base in range(0, win_rows - CONV_CHUNK + 1, SUBLANES):
                    t = base + phase - lead
                    if 0 <= t < CV_WIDTH:
                        term = shifted[base:base + CONV_CHUNK, :] * w_ref[0, t:t + 1, cs]
                        acc = term if acc is None else acc + term
            cols.append(acc)
        zc = jnp.concatenate(cols, axis=1) + cb_ref[0]
        mu = jnp.mean(zc, axis=-1, keepdims=True)
        cen = zc - mu
        var = jnp.mean(cen * cen, axis=-1, keepdims=True)
        y = cen * lax.rsqrt(var + EPS) * lg_ref[0] + lb_ref[0]
        o_ref[r0:r0 + CONV_CHUNK, :] = (y * jax.nn.sigmoid(y)).astype(o_ref.dtype)
    hist_out_ref[0] = scr[rows:rows + HC_PAD, :]


def _mixc_prompt_kernel(ca_ref, cg_ref, cap_ref, cgp_ref, w_ref, cb_ref, lg_ref, lb_ref,
                        o_ref, hist_out_ref, scr):
    first = pl.program_id(0) % (SEQ // MIX_ROWS) == 0
    scr[0:HC_PAD, :] = jnp.where(first, 0.0, _glu(cap_ref[...], cgp_ref[...]))
    _mixc_body(ca_ref, cg_ref, w_ref, cb_ref, lg_ref, lb_ref, o_ref, hist_out_ref, scr, MIX_ROWS)


def _mixc_sample_kernel(ca_ref, cg_ref, hist_ref, w_ref, cb_ref, lg_ref, lb_ref, y_hbm,
                        o_ref, hist_out_ref, scr):
    del y_hbm
    scr[0:HC_PAD, :] = hist_ref[0]
    _mixc_body(ca_ref, cg_ref, w_ref, cb_ref, lg_ref, lb_ref, o_ref, hist_out_ref, scr, DEC_SEQ)


def _mixc_prompt(z_rest, w, cb, lg, lb, layer):
    tiles_per_seq = SEQ // MIX_ROWS
    cur = lambda c: pl.BlockSpec((MIX_ROWS, WIDTH_CV), lambda i: (i, c))
    prev = lambda c: pl.BlockSpec((HC_PAD, WIDTH_CV),
                                  lambda i: (jnp.maximum(i * (MIX_ROWS // HC_PAD) - 1, 0), c))
    vec = _layer_rows(1, WIDTH_CV, layer)
    return pl.pallas_call(
        _mixc_prompt_kernel,
        grid=(ROWS_P // MIX_ROWS,),
        in_specs=[cur(ZR_CA), cur(ZR_CG), prev(ZR_CA), prev(ZR_CG),
                  _layer_rows(CV_WIDTH, WIDTH_CV, layer), vec, vec, vec],
        out_specs=[pl.BlockSpec((MIX_ROWS, WIDTH_CV), lambda i: (i, 0)),
                   pl.BlockSpec((1, HC_PAD, WIDTH_CV), lambda i: (i // tiles_per_seq, 0, 0))],
        out_shape=[jax.ShapeDtypeStruct((ROWS, WIDTH_CV), BF16),
                   jax.ShapeDtypeStruct((BATCH, HC_PAD, WIDTH_CV), F32)],
        scratch_shapes=[pltpu.VMEM((HC_PAD + MIX_ROWS, WIDTH_CV), F32)],
        compiler_params=_params("arbitrary"),
        name="mixc_prompt",
    )(z_rest, z_rest, z_rest, z_rest, w, cb, lg, lb)


def _mixc_sample(z_rest, hist, w, cb, lg, lb, layer, y_cv):
    blk0 = ROWS_P // DEC_SEQ
    cur = lambda c: pl.BlockSpec((DEC_SEQ, WIDTH_CV), lambda b: (b + blk0, c))
    vec = _layer_rows(1, WIDTH_CV, layer)
    return pl.pallas_call(
        _mixc_sample_kernel,
        grid=(DEC_BATCH,),
        in_specs=[cur(ZR_CA), cur(ZR_CG),
                  pl.BlockSpec((1, HC_PAD, WIDTH_CV), lambda b: (b, 0, 0)),
                  _layer_rows(CV_WIDTH, WIDTH_CV, layer), vec, vec, vec,
                  _ANY],
        out_specs=[pl.BlockSpec((DEC_SEQ, WIDTH_CV), lambda b: (b + blk0, 0)),
                   pl.BlockSpec((1, HC_PAD, WIDTH_CV), lambda b: (b, 0, 0))],
        out_shape=[jax.ShapeDtypeStruct((ROWS, WIDTH_CV), BF16),
                   jax.ShapeDtypeStruct((DEC_BATCH, HC_PAD, WIDTH_CV), F32)],
        scratch_shapes=[pltpu.VMEM((HC_PAD + DEC_SEQ, WIDTH_CV), F32)],
        input_output_aliases={7: 0},
        compiler_params=_params("arbitrary"),
        name="mixc_sample",
    )(z_rest, z_rest, hist, w, cb, lg, lb, y_cv)


def _outproj_kernel(ya_ref, yb_ref, yc_ref, w_ref, xp_ref, xs_ref, o_ref, wb_ref):
    _stationary_weights(w_ref, wb_ref)
    a0, a1 = WIDTH_ATT, WIDTH_ATT + WIDTH_SC
    acc = jnp.dot(ya_ref[...], wb_ref[:a0, :], preferred_element_type=F32)
    acc += jnp.dot(yb_ref[...], wb_ref[a0:a1, :], preferred_element_type=F32)
    acc += jnp.dot(yc_ref[...], wb_ref[a1:, :], preferred_element_type=F32)
    x = jnp.where(pl.program_id(1) < P_TILES, xp_ref[...], xs_ref[...])
    o_ref[...] = x + acc


def _outproj(y_att, y_sc, y_cv, w_out, layer, x_p, x_s, s_blk):
    row = lambda width: pl.BlockSpec((TM, width), lambda j, i: (i, 0))
    return pl.pallas_call(
        _outproj_kernel,
        grid=(D_MODEL // TN, ROWS // TM),
        in_specs=[row(WIDTH_ATT), row(WIDTH_SC), row(WIDTH_CV),
                  pl.BlockSpec((1, D_MODEL, TN), lambda j, i: (layer, 0, j), pipeline_mode=pl.Buffered(1)),
                  pl.BlockSpec((TM, TN), lambda j, i: (jnp.minimum(i, P_TILES - 1), j)),
                  pl.BlockSpec((TM, TN), lambda j, i: (s_blk, j))],
        out_specs=pl.BlockSpec((TM, TN), lambda j, i: (i, j)),
        out_shape=jax.ShapeDtypeStruct((ROWS, D_MODEL), F32),
        scratch_shapes=[pltpu.VMEM((D_MODEL, TN), BF16)],
        compiler_params=_params("arbitrary", "arbitrary"),
        name="outproj",
    )(y_att, y_sc, y_cv, w_out, x_p, x_s)


def _rmsnorm_rows(src_ref, g_ref, dst_ref, copy_ref=None):
    col_blocks = [slice(c0, c0 + NORM_COLS) for c0 in range(0, D_MODEL, NORM_COLS)]

    def chunk(c, carry):
        rows = pl.ds(pl.multiple_of(c * NORM_CHUNK, NORM_CHUNK), NORM_CHUNK)
        ss = None
        for cs in col_blocks:
            x = src_ref[rows, cs]
            ss = x * x if ss is None else ss + x * x
        inv = lax.rsqrt(jnp.sum(ss, axis=-1, keepdims=True) * (1.0 / D_MODEL) + EPS)
        for cs in col_blocks:
            x = src_ref[rows, cs]
            dst_ref[rows, cs] = (x * inv * g_ref[0, :, cs]).astype(dst_ref.dtype)
            if copy_ref is not None:
                copy_ref[rows, cs] = x
        return carry

    lax.fori_loop(0, TM // NORM_CHUNK, chunk, 0, unroll=4)


def _ffn_kernel(x_ref, g_ref, wgu_ref, wd_ref, o_ref, h_scr):
    @pl.when(pl.program_id(1) == 0)
    def _():
        _rmsnorm_rows(x_ref, g_ref, h_scr, copy_ref=o_ref)

    gu = jnp.dot(h_scr[...], wgu_ref[0], preferred_element_type=F32)
    g = gu[:, :TF]
    u = gu[:, TF:]
    act = (g * jax.nn.sigmoid(g) * u).astype(BF16)
    o_ref[...] += jnp.dot(act, wd_ref[...], preferred_element_type=F32)


def _ffn(x, g, w_gu, w_down_b, layer):
    return pl.pallas_call(
        _ffn_kernel,
        grid=(ROWS // TM, N_F),
        in_specs=[pl.BlockSpec((TM, D_MODEL), lambda i, f: (i, 0), pipeline_mode=pl.Buffered(1)),
                  pl.BlockSpec((1, 1, D_MODEL), lambda i, f: (layer, 0, 0)),
                  pl.BlockSpec((1, D_MODEL, 2 * TF), lambda i, f: (f, 0, 0)),
                  pl.BlockSpec((TF, D_MODEL), lambda i, f: (f, 0))],
        out_specs=pl.BlockSpec((TM, D_MODEL), lambda i, f: (i, 0)),
        out_shape=jax.ShapeDtypeStruct((ROWS, D_MODEL), F32),
        scratch_shapes=[pltpu.VMEM((TM, D_MODEL), BF16)],
        compiler_params=_params("parallel", "arbitrary", vmem=VMEM_LIMIT_FFN),
        name="ffn",
    )(x, g, w_gu, w_down_b)


def _rel_bias_tile(rel_bias_l, rows, cols, n_past):
    n_heads = rel_bias_l.shape[0]
    period = rows + cols
    k = jnp.arange(period)
    dist = n_past - jnp.where(k < cols, k, k - period)
    u = rel_bias_l[:, jnp.clip(dist, -MAX_REL, MAX_REL) + MAX_REL].astype(F32)
    flat = jnp.broadcast_to(u[:, None, :], (n_heads, rows, period)).reshape(n_heads, rows * period)
    return flat[:, :rows * (period - 1)].reshape(n_heads, rows, period - 1)[:, :, :cols]


def _prompt_bias(rel_bias_l):
    bias = _rel_bias_tile(rel_bias_l, ATT_Q, ATT_KB * ATT_Q, ATT_PAST)
    r = jnp.arange(ATT_Q)[:, None]
    c = jnp.arange(ATT_KB * ATT_Q)[None, :]
    q_chunk = r // CHUNK + LEFT_CHUNKS
    k_chunk = c // CHUNK
    band = (k_chunk <= q_chunk) & (k_chunk >= q_chunk - LEFT_CHUNKS)
    variant = jnp.arange(ATT_KB)[:, None, None]
    valid = band[None] & (c[None] // ATT_Q >= ATT_KB - 1 - variant)
    return jnp.where(valid[:, None], bias[None], NEG)


def _pad_hist(hist, pad_to):
    return jnp.pad(hist, ((0, 0), (pad_to - hist.shape[1], 0), (0, 0)))


def kernel(x_prompt, x_sample, cache_attn_k, cache_attn_v, cache_conv_b, cache_conv_c, norm_mix_g, w_in,
           rel_bias, conv_b_w, conv_c_w, conv_c_b, ln_c_g, ln_c_b, w_out, norm_ffn_g, w_ffn_gate, w_ffn_up,
           w_ffn_down, final_norm_g):
    n_cache = cache_attn_k.shape[2]
    x_p = x_prompt.reshape(ROWS_P, D_MODEL)
    x_s = x_sample.reshape(ROWS_S, D_MODEL)
    g_ffn = norm_ffn_g.reshape(DEPTH, 1, D_MODEL)
    cb, lg, lb = (a.reshape(DEPTH, 1, WIDTH_CV) for a in (conv_c_b, ln_c_g, ln_c_b))

    hists = {k: [] for k in ("pb", "pc", "sb", "sc")}
    pk = pv = sk = sv = None
    x = None
    h = _rmsnorm(x_p, norm_mix_g[0], BF16, 0, ROWS_P, ROWS)
    h = _rmsnorm(x_s, norm_mix_g[0], BF16, 0, ROWS_S, ROWS, dst_row0=ROWS_P, dst=h)
    for l in range(DEPTH):
        z_k, pk, sk = _inproj_kv(h, w_in, l, WIDTH_ATT, pk, sk)
        z_v, pv, sv = _inproj_kv(h, w_in, l, 2 * WIDTH_ATT, pv, sv)
        z_rest, w_gu, w_down_b = _inproj_rest(h, w_in, w_ffn_gate, w_ffn_up, w_ffn_down, l)

        y_att = _attn_prompt(z_rest, z_k, z_v, _prompt_bias(rel_bias[l]))
        bias_s = _rel_bias_tile(rel_bias[l], DEC_SEQ, n_cache + DEC_SEQ, n_cache)
        y_att = _attn_sample(z_rest, z_k, z_v, cache_attn_k, cache_attn_v, l,
                             bias_s[:, :, :n_cache], bias_s[:, :, n_cache:], y_att)

        y_sc, hb_p = _mixb_prompt(z_rest, conv_b_w, l)
        y_sc, hb_s = _mixb_sample(z_rest, _pad_hist(cache_conv_b[l], HB_PAD), conv_b_w, l, y_sc)
        y_cv, hc_p = _mixc_prompt(z_rest, conv_c_w, cb, lg, lb, l)
        y_cv, hc_s = _mixc_sample(z_rest, _pad_hist(cache_conv_c[l], HC_PAD), conv_c_w, cb, lg, lb, l, y_cv)

        if l == 0:
            x = _outproj(y_att, y_sc, y_cv, w_out, l, x_p, x_s, 0)
        else:
            x = _outproj(y_att, y_sc, y_cv, w_out, l, x, x, P_TILES)
        x = _ffn(x, g_ffn, w_gu, w_down_b, l)
        if l + 1 < DEPTH:
            h = _rmsnorm(x, norm_mix_g[l + 1], BF16, 0, ROWS, ROWS)

        hists["pb"].append(hb_p[:, HB_PAD - (SC_WIDTH - 1):])
        hists["pc"].append(hc_p[:, HC_PAD - (CV_WIDTH - 1):])
        hists["sb"].append(hb_s[:, HB_PAD - (SC_WIDTH - 1):])
        hists["sc"].append(hc_s[:, HC_PAD - (CV_WIDTH - 1):])

    y_prompt = _rmsnorm(x, final_norm_g, F32, 0, ROWS_P, ROWS_P).reshape(BATCH, SEQ, D_MODEL)
    y_sample = _rmsnorm(x, final_norm_g, F32, ROWS_P, ROWS_S, ROWS_S).reshape(DEC_BATCH, DEC_SEQ, D_MODEL)
    st = {k: jnp.stack(v) for k, v in hists.items()}
    heads_p = (DEPTH, BATCH, N_KEEP, N_HEADS, HEAD_DIM)
    heads_s = (DEPTH, DEC_BATCH, DEC_SEQ, N_HEADS, HEAD_DIM)
    return (y_prompt, y_sample, pk.reshape(heads_p), pv.reshape(heads_p), st["pb"], st["pc"],
            sk.reshape(heads_s), sv.reshape(heads_s), st["sb"], st["sc"])
```
